```python
import math
import jax
import jax.numpy as jnp
from jax import lax
import numpy as np

D_MODEL = 2048
BATCH = 4
SEQ = 2048
DEPTH = 2

CHUNK = 128
EPS = 1e-6
D_FF = 5632
N_BRANCH = 4
BRANCH_WIDTH = 1024

A_GROUPS = 8
A_GROUP_DIM = BRANCH_WIDTH // A_GROUPS

B_HEAD_DIM = 64
B_HEADS = BRANCH_WIDTH // B_HEAD_DIM
B_GROUPS = 2
B_STATE = 128
B_CONV = 4
B_CONV_DIM = BRANCH_WIDTH + 2 * B_GROUPS * B_STATE

C_HEADS = 8
C_HEAD_QK = 64
C_QK = C_HEADS * C_HEAD_QK
C_HEAD_V = BRANCH_WIDTH // C_HEADS
ROPE_BASE = 10000.0

D_HEADS = 8
D_HEAD_DIM = BRANCH_WIDTH // D_HEADS

A_IN = 2 * BRANCH_WIDTH
B_IN = BRANCH_WIDTH + B_CONV_DIM + B_HEADS
C_IN = 2 * C_QK + 2 * BRANCH_WIDTH
D_IN = 3 * BRANCH_WIDTH + D_HEADS
G_IN = N_BRANCH * D_MODEL
N_IN = A_IN + B_IN + C_IN + D_IN + G_IN
SPLIT_POINTS = (A_IN, A_IN + B_IN, A_IN + B_IN + C_IN, A_IN + B_IN + C_IN + D_IN)

kernel_name = 'hybrid_gated_sgu_ssd_retention_fox_macaron'


def rms_norm(x, g):
    xf = x.astype(jnp.float32)
    y = xf * lax.rsqrt(jnp.mean(xf * xf, axis=-1, keepdims=True) + EPS)
    return (y * g.astype(jnp.float32)).astype(x.dtype)


def swiglu(h, w_in, w_out):
    gate, up = jnp.split(h @ w_in, 2, axis=-1)
    return (jax.nn.silu(gate) * up) @ w_out


def causal_mask():
    return jnp.tril(jnp.ones((CHUNK, CHUNK), dtype=bool))


def chunked_sgu(p, ln_g, w_s, b_s):
    bsz, seq, _ = p.shape
    nc = seq // CHUNK
    u, v = jnp.split(jax.nn.gelu(p), 2, axis=-1)
    vf = v.astype(jnp.float32)
    mu = jnp.mean(vf, axis=-1, keepdims=True)
    var = jnp.mean(jnp.square(vf - mu), axis=-1, keepdims=True)
    vn = ((vf - mu) * lax.rsqrt(var + EPS) * ln_g.astype(jnp.float32)).astype(p.dtype)
    vc = vn.reshape(bsz, nc, CHUNK, A_GROUPS, A_GROUP_DIM)
    w = jnp.where(causal_mask()[None], w_s, jnp.zeros_like(w_s))
    mixed = jnp.einsum('gts,bnsgc->bntgc', w, vc) + b_s.T[None, None, :, :, None]
    return u * mixed.reshape(bsz, seq, BRANCH_WIDTH)


def causal_depthwise_conv(x, w, b):
    k = w.shape[0]
    y = lax.conv_general_dilated(
        x, w[:, None, :].astype(x.dtype), window_strides=(1,), padding=((k - 1, 0),),
        dimension_numbers=('NWC', 'WIO', 'NWC'), feature_group_count=x.shape[-1])
    return y + b.astype(x.dtype)


def ssd_mixer(p, conv_w, conv_b, dt_bias, a_log, d_skip, norm_g):
    f32 = jnp.float32
    bsz, seq, _ = p.shape
    nc = seq // CHUNK
    hpg = B_HEADS // B_GROUPS
    z, xbc, dt = jnp.split(p, [BRANCH_WIDTH, BRANCH_WIDTH + B_CONV_DIM], axis=-1)
    xbc = jax.nn.silu(causal_depthwise_conv(xbc, conv_w, conv_b))
    xs, bm, cm = jnp.split(xbc, [BRANCH_WIDTH, BRANCH_WIDTH + B_GROUPS * B_STATE], axis=-1)
    dt = jax.nn.softplus(dt.astype(f32) + dt_bias.astype(f32))
    a = -jnp.exp(a_log.astype(f32)).reshape(B_GROUPS, hpg)
    xh = xs.astype(f32).reshape(bsz, nc, CHUNK, B_GROUPS, hpg, B_HEAD_DIM)
    dtc = dt.reshape(bsz, nc, CHUNK, B_GROUPS, hpg)
    bmc = bm.astype(f32).reshape(bsz, nc, CHUNK, B_GROUPS, B_STATE)
    cmc = cm.astype(f32).reshape(bsz, nc, CHUNK, B_GROUPS, B_STATE)
    acum = jnp.cumsum(dtc * a, axis=2)
    xdt = xh * dtc[..., None]
    seg = acum[:, :, :, None] - acum[:, :, None, :]
    mask = causal_mask()[:, :, None, None]
    decay = jnp.exp(jnp.where(mask, seg, -jnp.inf))
    cb = jnp.einsum('bctgn,bcsgn->bctsg', cmc, bmc)
    y_diag = jnp.einsum('bctsgh,bcsghp->bctghp', cb[..., None] * decay, xdt)
    decay_end = jnp.exp(acum[:, :, -1:] - acum)
    states = jnp.einsum('bcsgn,bcsghp->bcghpn', bmc, xdt * decay_end[..., None])
    chunk_decay = jnp.exp(acum[:, :, -1])

    def step(s_prev, inp):
        st, cd = inp
        return s_prev * cd[..., None, None] + st, s_prev

    init = jnp.zeros((bsz, B_GROUPS, hpg, B_HEAD_DIM, B_STATE), f32)
    _, prev = lax.scan(step, init, (jnp.moveaxis(states, 1, 0), jnp.moveaxis(chunk_decay, 1, 0)))
    prev = jnp.moveaxis(prev, 0, 1)
    y_off = jnp.einsum('bctgn,bcghpn->bctghp', cmc, prev) * jnp.exp(acum)[..., None]
    y = y_diag + y_off + xh * d_skip.astype(f32).reshape(B_GROUPS, hpg)[..., None]
    y = y.reshape(bsz, seq, BRANCH_WIDTH) * jax.nn.silu(z.astype(f32))
    yg = y.reshape(bsz, seq, B_GROUPS, BRANCH_WIDTH // B_GROUPS)
    yg = yg * lax.rsqrt(jnp.mean(yg * yg, axis=-1, keepdims=True) + EPS)
    y = yg.reshape(bsz, seq, BRANCH_WIDTH) * norm_g.astype(f32)
    return y.astype(p.dtype)


def rotary(x, pos):
    half = x.shape[-1] // 2
    inv_freq = 1.0 / (ROPE_BASE ** (jnp.arange(half, dtype=jnp.float32) / half))
    ang = pos[:, None] * inv_freq[None, :]
    cos = jnp.cos(ang)[None, :, None, :]
    sin = jnp.sin(ang)[None, :, None, :]
    x1, x2 = x[..., :half], x[..., half:]
    return jnp.concatenate([x1 * cos - x2 * sin, x1 * sin + x2 * cos], axis=-1)


def retention_mixer(p):
    f32 = jnp.float32
    bsz, seq, _ = p.shape
    nc = seq // CHUNK
    q, k, v, g = jnp.split(p, [C_QK, 2 * C_QK, 2 * C_QK + BRANCH_WIDTH], axis=-1)
    pos = jnp.arange(seq, dtype=f32)
    q = rotary(q.astype(f32).reshape(bsz, seq, C_HEADS, C_HEAD_QK), pos)
    k = rotary(k.astype(f32).reshape(bsz, seq, C_HEADS, C_HEAD_QK), pos) * (C_HEAD_QK ** -0.5)
    qc = q.reshape(bsz, nc, CHUNK, C_HEADS, C_HEAD_QK)
    kc = k.reshape(bsz, nc, CHUNK, C_HEADS, C_HEAD_QK)
    vc = v.astype(f32).reshape(bsz, nc, CHUNK, C_HEADS, C_HEAD_V)
    log_gamma = jnp.log(1.0 - 2.0 ** (-5.0 - jnp.arange(C_HEADS, dtype=f32)))
    idx = jnp.arange(CHUNK, dtype=f32)
    mask = causal_mask()
    rel = jnp.where(mask, idx[:, None] - idx[None, :], 0.0)
    intra_decay = jnp.exp(rel[..., None] * log_gamma) * mask[..., None]
    scores = jnp.einsum('bcthd,bcshd->bchts', qc, kc) * intra_decay.transpose(2, 0, 1)
    y_intra = jnp.einsum('bchts,bcshv->bcthv', scores, vc)
    k_decay = jnp.exp((CHUNK - 1.0 - idx)[:, None] * log_gamma)
    states = jnp.einsum('bcshd,bcshv->bchdv', kc * k_decay[..., None], vc)
    chunk_decay = jnp.exp(CHUNK * log_gamma)

    def step(s_prev, st):
        return s_prev * chunk_decay[:, None, None] + st, s_prev

    init = jnp.zeros((bsz, C_HEADS, C_HEAD_QK, C_HEAD_V), f32)
    _, prev = lax.scan(step, init, jnp.moveaxis(states, 1, 0))
    prev = jnp.moveaxis(prev, 0, 1)
    q_decay = jnp.exp((idx + 1.0)[:, None] * log_gamma)
    y_cross = jnp.einsum('bcthd,bchdv->bcthv', qc * q_decay[..., None], prev)
    y = y_intra + y_cross
    mu = jnp.mean(y, axis=-1, keepdims=True)
    var = jnp.mean(jnp.square(y - mu), axis=-1, keepdims=True)
    y = ((y - mu) * lax.rsqrt(var + EPS)).reshape(bsz, seq, BRANCH_WIDTH)
    return (jax.nn.silu(g.astype(f32)) * y).astype(p.dtype)


def forgetting_attention(p, f_bias):
    f32 = jnp.float32
    bsz, seq, _ = p.shape
    nb = seq // CHUNK
    q, k, v, fl = jnp.split(p, [BRANCH_WIDTH, 2 * BRANCH_WIDTH, 3 * BRANCH_WIDTH], axis=-1)
    q = q.reshape(bsz, seq, D_HEADS, D_HEAD_DIM)
    k = k.reshape(bsz, seq, D_HEADS, D_HEAD_DIM)
    v = v.reshape(bsz, seq, D_HEADS, D_HEAD_DIM)
    log_f = jax.nn.log_sigmoid(fl.astype(f32) + f_bias.astype(f32))
    cum = jnp.cumsum(log_f, axis=1).transpose(0, 2, 1)
    scale = D_HEAD_DIM ** -0.5
    kpos = jnp.arange(seq)

    def block(i):
        start = i * CHUNK
        qb = lax.dynamic_slice_in_dim(q, start, CHUNK, axis=1)
        cq = lax.dynamic_slice_in_dim(cum, start, CHUNK, axis=2)
        s = jnp.einsum('bqhd,bkhd->bhqk', qb, k).astype(f32) * scale
        s = s + cq[..., None] - cum[:, :, None, :]
        qpos = start + jnp.arange(CHUNK)
        s = jnp.where((kpos[None, :] <= qpos[:, None])[None, None], s, -jnp.inf)
        w = jax.nn.softmax(s, axis=-1)
        return jnp.einsum('bhqk,bkhd->bqhd', w.astype(v.dtype), v)

    out = lax.map(block, jnp.arange(nb))
    return out.transpose(1, 0, 2, 3, 4).reshape(bsz, seq, BRANCH_WIDTH)


def setup_inputs(seed: int = 0) -> dict:
    key = jax.random.key(seed)
    ks = jax.random.split(key, 24)
    f32 = jnp.float32

    def nrm(k, shape, scale):
        return jax.random.normal(k, shape, f32) * scale

    def gain(k, shape):
        return 1.0 + 0.05 * jax.random.normal(k, shape, f32)

    dt0 = jnp.exp(jax.random.uniform(ks[11], (DEPTH, B_HEADS), f32, math.log(1e-3), math.log(1e-1)))
    return {
        'x': jax.random.normal(ks[0], (BATCH, SEQ, D_MODEL), f32),
        'ffn1_norm': gain(ks[1], (DEPTH, D_MODEL)),
        'ffn1_w_in': nrm(ks[2], (DEPTH, D_MODEL, 2 * D_FF), D_MODEL ** -0.5),
        'ffn1_w_out': nrm(ks[3], (DEPTH, D_FF, D_MODEL), D_FF ** -0.5),
        'mix_norm': gain(ks[4], (DEPTH, D_MODEL)),
        'w_mix_in': nrm(ks[5], (DEPTH, D_MODEL, N_IN), D_MODEL ** -0.5),
        'sgu_norm': gain(ks[6], (DEPTH, BRANCH_WIDTH)),
        'sgu_w': nrm(ks[7], (DEPTH, A_GROUPS, CHUNK, CHUNK), CHUNK ** -0.5),
        'sgu_b': gain(ks[8], (DEPTH, A_GROUPS, CHUNK)),
        'conv_w': nrm(ks[9], (DEPTH, B_CONV, B_CONV_DIM), B_CONV ** -0.5),
        'conv_b': nrm(ks[10], (DEPTH, B_CONV_DIM), 0.02),
        'dt_bias': dt0 + jnp.log(-jnp.expm1(-dt0)),
        'a_log': jnp.log(jax.random.uniform(ks[12], (DEPTH, B_HEADS), f32, 1.0, 16.0)),
        'd_skip': gain(ks[13], (DEPTH, B_HEADS)),
        'ssm_norm': gain(ks[14], (DEPTH, BRANCH_WIDTH)),
        'forget_bias': jax.random.uniform(ks[15], (DEPTH, D_HEADS), f32, 2.0, 5.0),
        'w_branch': nrm(ks[16], (DEPTH, N_BRANCH, BRANCH_WIDTH, D_MODEL), BRANCH_WIDTH ** -0.5),
        'w_mix_out': nrm(ks[17], (DEPTH, D_MODEL, D_MODEL), D_MODEL ** -0.5),
        'ffn2_norm': gain(ks[18], (DEPTH, D_MODEL)),
        'ffn2_w_in': nrm(ks[19], (DEPTH, D_MODEL, 2 * D_FF), D_MODEL ** -0.5),
        'ffn2_w_out': nrm(ks[20], (DEPTH, D_FF, D_MODEL), D_FF ** -0.5),
        'final_norm': gain(ks[21], (D_MODEL,)),
    }


def reference(x, ffn1_norm, ffn1_w_in, ffn1_w_out, mix_norm, w_mix_in, sgu_norm, sgu_w, sgu_b,
              conv_w, conv_b, dt_bias, a_log, d_skip, ssm_norm, forget_bias, w_branch, w_mix_out,
              ffn2_norm, ffn2_w_in, ffn2_w_out, final_norm):
    for l in range(DEPTH):
        x = x + 0.5 * swiglu(rms_norm(x, ffn1_norm[l]), ffn1_w_in[l], ffn1_w_out[l])
        h = rms_norm(x, mix_norm[l])
        p_a, p_b, p_c, p_d, p_g = jnp.split(h @ w_mix_in[l], SPLIT_POINTS, axis=-1)
        y_a = chunked_sgu(p_a, sgu_norm[l], sgu_w[l], sgu_b[l])
        y_b = ssd_mixer(p_b, conv_w[l], conv_b[l], dt_bias[l], a_log[l], d_skip[l], ssm_norm[l])
        y_c = retention_mixer(p_c)
        y_d = forgetting_attention(p_d, forget_bias[l])
        gates = jax.nn.sigmoid(p_g.astype(jnp.float32)).astype(x.dtype)
        merged = gates[..., :D_MODEL] * (y_a @ w_branch[l, 0])
        merged = merged + gates[..., D_MODEL:2 * D_MODEL] * (y_b @ w_branch[l, 1])
        merged = merged + gates[..., 2 * D_MODEL:3 * D_MODEL] * (y_c @ w_branch[l, 2])
        merged = merged + gates[..., 3 * D_MODEL:] * (y_d @ w_branch[l, 3])
        x = x + merged @ w_mix_out[l]
        x = x + 0.5 * swiglu(rms_norm(x, ffn2_norm[l]), ffn2_w_in[l], ffn2_w_out[l])
    return rms_norm(x, final_norm)
```

```python
import functools
import math

import jax
import jax.numpy as jnp
from jax import lax
from jax.experimental import pallas as pl
from jax.experimental.pallas import tpu as pltpu

F32 = jnp.float32
BF16 = jnp.bfloat16
HIGHEST = lax.Precision.HIGHEST

EPS = 1e-6
CHUNK = 128
LANES = 128
VMEM_LIMIT = 56 * 1024 * 1024

D_MODEL = 2048
D_FF = 5632
WIDTH = 1024
A_GROUPS = 8
B_HEADS = 16
B_HEAD_DIM = 64
B_STATE = 128
B_GROUPS = 2
B_CONV = 4
B_CONV_DIM = WIDTH + 2 * B_GROUPS * B_STATE
C_HEADS = 8
C_HEAD_QK = 64
C_QK = C_HEADS * C_HEAD_QK
D_HEADS = 8
D_HEAD_DIM = 128
ROPE_BASE = 10000.0

COL_DT = 2 * WIDTH + WIDTH + B_CONV_DIM
COL_C = COL_DT + B_HEADS
COL_FL = COL_C + 2 * C_QK + 2 * WIDTH + 3 * WIDTH
COL_G = COL_FL + D_HEADS
N_AB = COL_DT
N_CD = COL_FL - COL_C
SMALL_FL = 0
SMALL_DT = D_HEADS


def _params(*sem):
    return pltpu.CompilerParams(dimension_semantics=sem, vmem_limit_bytes=VMEM_LIMIT)


def _dot(a, b):
    return jnp.dot(a.astype(BF16), b.astype(BF16), preferred_element_type=F32)


def _dot_nt(a, b):
    return lax.dot_general(a.astype(BF16), b.astype(BF16), (((1,), (1,)), ((), ())),
                           preferred_element_type=F32)


def _dot_f32(a, b):
    return jnp.dot(a, b, precision=HIGHEST, preferred_element_type=F32)


def _silu(x):
    return x * jax.nn.sigmoid(x)


def _softplus(x):
    return jnp.maximum(x, 0.0) + jnp.log1p(jnp.exp(-jnp.abs(x)))


def _lower_tri(n):
    return lax.broadcasted_iota(jnp.int32, (n, n), 1) <= lax.broadcasted_iota(jnp.int32, (n, n), 0)


def _rmsnorm_rows(x, g):
    return x * lax.rsqrt(jnp.mean(x * x, axis=-1, keepdims=True) + EPS) * g


def _rmsnorm_body(x_ref, g_ref, o_ref):
    o_ref[...] = _rmsnorm_rows(x_ref[...], g_ref[...]).astype(o_ref.dtype)


def rmsnorm(x, g, out_dtype, tm=512):
    m, d = x.shape
    return pl.pallas_call(
        _rmsnorm_body,
        out_shape=jax.ShapeDtypeStruct((m, d), out_dtype),
        grid=(m // tm,),
        in_specs=[pl.BlockSpec((tm, d), lambda i: (i, 0)), pl.BlockSpec((1, d), lambda i: (0, 0))],
        out_specs=pl.BlockSpec((tm, d), lambda i: (i, 0)),
        compiler_params=_params("parallel"),
        name="rmsnorm",
    )(x, g.reshape(1, d))


def _ffn_body(h_ref, x_ref, gn_ref, wg_ref, wu_ref, wo_ref, *refs, nf, emit_x):
    if emit_x:
        xo_ref, ho_ref, acc_ref = refs
    else:
        ho_ref, acc_ref = refs
    f = pl.program_id(1)

    @pl.when(f == 0)
    def _():
        acc_ref[...] = jnp.zeros_like(acc_ref)

    h = h_ref[...]
    gate = _dot(h, wg_ref[...])
    up = _dot(h, wu_ref[...])
    acc_ref[...] += _dot(_silu(gate) * up, wo_ref[...])

    @pl.when(f == nf - 1)
    def _():
        x_new = x_ref[...] + 0.5 * acc_ref[...]
        if emit_x:
            xo_ref[...] = x_new
        ho_ref[...] = _rmsnorm_rows(x_new, gn_ref[...]).astype(ho_ref.dtype)


def ffn(h, x, w_in, w_out, layer, g_next, *, emit_x, h_dtype, tm=512, tf=256):
    m, d = x.shape
    nf = D_FF // tf
    out_shape = [jax.ShapeDtypeStruct((m, d), h_dtype)]
    out_specs = [pl.BlockSpec((tm, d), lambda i, f: (i, 0))]
    if emit_x:
        out_shape.insert(0, jax.ShapeDtypeStruct((m, d), F32))
        out_specs.insert(0, pl.BlockSpec((tm, d), lambda i, f: (i, 0)))
    res = pl.pallas_call(
        functools.partial(_ffn_body, nf=nf, emit_x=emit_x),
        out_shape=out_shape,
        grid=(m // tm, nf),
        in_specs=[
            pl.BlockSpec((tm, d), lambda i, f: (i, 0)),
            pl.BlockSpec((tm, d), lambda i, f: (i, 0)),
            pl.BlockSpec((1, d), lambda i, f: (0, 0)),
            pl.BlockSpec((None, d, tf), lambda i, f: (layer, 0, f)),
            pl.BlockSpec((None, d, tf), lambda i, f: (layer, 0, nf + f)),
            pl.BlockSpec((None, tf, d), lambda i, f: (layer, f, 0)),
        ],
        out_specs=out_specs,
        scratch_shapes=[pltpu.VMEM((tm, d), F32)],
        compiler_params=_params("parallel", "arbitrary"),
        name="ffn",
    )(h, x, g_next.reshape(1, d), w_in, w_in, w_out)
    return res if emit_x else res[0]


def _matmul_body(h_ref, w_ref, o_ref, *, act):
    acc = _dot(h_ref[...], w_ref[...])
    if act == "sigmoid":
        acc = jax.nn.sigmoid(acc)
    o_ref[...] = acc.astype(o_ref.dtype)


def matmul(h, w, n, *, layer=None, col_block=0, act=None, out_dtype=F32, tm=1024, tn=512):
    m, k = h.shape
    tn = min(tn, n)
    if layer is None:
        w_spec = pl.BlockSpec((k, tn), lambda i, j: (0, col_block + j))
    else:
        w_spec = pl.BlockSpec((None, k, tn), lambda i, j: (layer, 0, col_block + j))
    return pl.pallas_call(
        functools.partial(_matmul_body, act=act),
        out_shape=jax.ShapeDtypeStruct((m, n), out_dtype),
        grid=(m // tm, n // tn),
        in_specs=[pl.BlockSpec((tm, k), lambda i, j: (i, 0)), w_spec],
        out_specs=pl.BlockSpec((tm, tn), lambda i, j: (i, j)),
        compiler_params=_params("parallel", "arbitrary"),
        name="proj",
    )(h, w)


def _sgu_body(p_ref, lng_ref, w_ref, b_ref, o_ref):
    p = jax.nn.gelu(p_ref[...])
    u = p[:, :WIDTH]
    v = p[:, WIDTH:]
    mu = jnp.mean(v, axis=-1, keepdims=True)
    var = jnp.mean(jnp.square(v - mu), axis=-1, keepdims=True)
    vn = (v - mu) * lax.rsqrt(var + EPS) * lng_ref[...]
    causal = _lower_tri(CHUNK)
    for g in range(A_GROUPS):
        cols = slice(g * LANES, (g + 1) * LANES)
        w = jnp.where(causal, w_ref[g], 0.0)
        mixed = _dot(w, vn[:, cols]) + b_ref[:, cols]
        o_ref[:, cols] = (u[:, cols] * mixed).astype(o_ref.dtype)


def sgu(p_ab, ln_g, w_s, b_full, layer):
    m = p_ab.shape[0]
    return pl.pallas_call(
        _sgu_body,
        out_shape=jax.ShapeDtypeStruct((m, WIDTH), BF16),
        grid=(m // CHUNK,),
        in_specs=[
            pl.BlockSpec((CHUNK, 2 * WIDTH), lambda c: (c, 0)),
            pl.BlockSpec((1, WIDTH), lambda c: (0, 0)),
            pl.BlockSpec((None, A_GROUPS, CHUNK, CHUNK), lambda c: (layer, 0, 0, 0)),
            pl.BlockSpec((CHUNK, WIDTH), lambda c: (0, 0)),
        ],
        out_specs=pl.BlockSpec((CHUNK, WIDTH), lambda c: (c, 0)),
        compiler_params=_params("parallel"),
        name="sgu",
    )(p_ab, ln_g.reshape(1, WIDTH), w_s, b_full)


def _ssd_body(z_ref, xbc_ref, small_ref, cw_ref, cb_ref, dtb_ref, alog_ref, dsk_ref, ng_ref, o_ref,
              xpad_ref, state_ref, y_ref):
    c = pl.program_id(1)

    @pl.when(c == 0)
    def _():
        xpad_ref[0:8, :] = jnp.zeros((8, B_CONV_DIM), F32)
        state_ref[...] = jnp.zeros_like(state_ref)

    xpad_ref[8:8 + CHUNK, :] = xbc_ref[...]
    conv = cb_ref[...]
    for j in range(B_CONV):
        conv = conv + cw_ref[j:j + 1, :] * xpad_ref[pl.ds(8 - (B_CONV - 1) + j, CHUNK), :]
    xpad_ref[0:8, :] = xpad_ref[CHUNK:CHUNK + 8, :]
    xbc = _silu(conv)
    xs = xbc[:, :WIDTH]
    bm = xbc[:, WIDTH:WIDTH + B_GROUPS * B_STATE]
    cm = xbc[:, WIDTH + B_GROUPS * B_STATE:]

    dt = _softplus(small_ref[...] + dtb_ref[...])
    a = -jnp.exp(alog_ref[...])
    causal = _lower_tri(CHUNK)
    acum = _dot_f32(causal.astype(F32), dt * a)
    acum_t = acum.T
    expand = (lax.broadcasted_iota(jnp.int32, (LANES, WIDTH), 0) - SMALL_DT
              == lax.broadcasted_iota(jnp.int32, (LANES, WIDTH), 1) // B_HEAD_DIM).astype(F32)
    dt_e = _dot_f32(dt, expand)
    eacum_e = _dot_f32(jnp.exp(acum), expand)
    dend_e = _dot_f32(jnp.exp(acum[CHUNK - 1:CHUNK, :] - acum), expand)
    xdt = xs * dt_e
    xend = xdt * dend_e
    lane = lax.broadcasted_iota(jnp.int32, (CHUNK, LANES), 1)
    low_half = lane < B_HEAD_DIM
    hpg = B_HEADS // B_GROUPS
    gw = WIDTH // B_GROUPS

    for g in range(B_GROUPS):
        bg = bm[:, g * B_STATE:(g + 1) * B_STATE]
        cg = cm[:, g * B_STATE:(g + 1) * B_STATE]
        cb = _dot_nt(cg, bg)
        gcols = slice(g * gw, (g + 1) * gw)
        prev = state_ref[:, gcols]
        y_off = _dot(cg, prev) * eacum_e[:, gcols]
        for pair in range(hpg // 2):
            pcols = slice(g * gw + pair * LANES, g * gw + (pair + 1) * LANES)
            x_pair = xdt[:, pcols]
            y_pair = jnp.zeros((CHUNK, LANES), F32)
            for half in range(2):
                hl = SMALL_DT + g * hpg + 2 * pair + half
                seg = acum[:, hl:hl + 1] - acum_t[hl:hl + 1, :]
                decay = jnp.exp(jnp.where(causal, seg, -jnp.inf))
                x_half = jnp.where(low_half if half == 0 else ~low_half, x_pair, 0.0)
                y_pair = y_pair + _dot(cb * decay, x_half)
            y_pair = y_pair + y_off[:, pair * LANES:(pair + 1) * LANES] + xs[:, pcols] * dsk_ref[:, pcols]
            y_ref[:, pcols] = y_pair
        st = _dot(bg.T, xend[:, gcols])
        state_ref[:, gcols] = prev * eacum_e[CHUNK - 1:CHUNK, gcols] + st

    y = y_ref[...] * _silu(z_ref[...])
    for g in range(B_GROUPS):
        gcols = slice(g * gw, (g + 1) * gw)
        yg = y[:, gcols]
        yg = yg * lax.rsqrt(jnp.mean(yg * yg, axis=-1, keepdims=True) + EPS)
        o_ref[:, gcols] = (yg * ng_ref[:, gcols]).astype(o_ref.dtype)


def ssd(p_ab, p_small, conv_w, conv_b, dtb_row, alog_row, dskip_row, norm_g, layer, bsz, seq):
    nc = seq // CHUNK
    m = bsz * seq
    row = lambda b, c: b * nc + c
    return pl.pallas_call(
        _ssd_body,
        out_shape=jax.ShapeDtypeStruct((m, WIDTH), BF16),
        grid=(bsz, nc),
        in_specs=[
            pl.BlockSpec((CHUNK, WIDTH), lambda b, c: (row(b, c), 2)),
            pl.BlockSpec((CHUNK, B_CONV_DIM), lambda b, c: (row(b, c), 2)),
            pl.BlockSpec((CHUNK, LANES), lambda b, c: (row(b, c), 0)),
            pl.BlockSpec((None, B_CONV, B_CONV_DIM), lambda b, c: (layer, 0, 0)),
            pl.BlockSpec((None, 1, B_CONV_DIM), lambda b, c: (layer, 0, 0)),
            pl.BlockSpec((1, LANES), lambda b, c: (0, 0)),
            pl.BlockSpec((1, LANES), lambda b, c: (0, 0)),
            pl.BlockSpec((1, WIDTH), lambda b, c: (0, 0)),
            pl.BlockSpec((1, WIDTH), lambda b, c: (0, 0)),
        ],
        out_specs=pl.BlockSpec((CHUNK, WIDTH), lambda b, c: (row(b, c), 0)),
        scratch_shapes=[pltpu.VMEM((CHUNK + 8, B_CONV_DIM), F32), pltpu.VMEM((B_STATE, WIDTH), F32),
                        pltpu.VMEM((CHUNK, WIDTH), F32)],
        compiler_params=_params("parallel", "arbitrary"),
        name="ssd",
    )(p_ab, p_ab, p_small, conv_w, conv_b.reshape(conv_b.shape[0], 1, B_CONV_DIM), dtb_row, alog_row,
      dskip_row, norm_g.reshape(1, WIDTH))


def _retention_tables(seq):
    half = C_HEAD_QK // 2
    inv_freq = 1.0 / (ROPE_BASE ** (jnp.arange(half, dtype=F32) / half))
    ang = jnp.arange(seq, dtype=F32)[:, None] * inv_freq[None, :]
    cos = jnp.tile(jnp.cos(ang), (1, 2 * LANES // C_HEAD_QK))
    sin = jnp.tile(jnp.concatenate([-jnp.sin(ang), jnp.sin(ang)], axis=1), (1, LANES // C_HEAD_QK))
    log_gamma = jnp.log(1.0 - 2.0 ** (-5.0 - jnp.arange(C_HEADS, dtype=F32)))
    idx = jnp.arange(CHUNK, dtype=F32)
    mask = idx[:, None] >= idx[None, :]
    rel = jnp.where(mask, idx[:, None] - idx[None, :], 0.0)
    intra = (jnp.exp(rel[..., None] * log_gamma) * mask[..., None]).transpose(2, 0, 1)
    k_decay = jnp.repeat(jnp.exp((CHUNK - 1.0 - idx)[:, None] * log_gamma), C_HEAD_QK, axis=1)
    q_decay = jnp.repeat(jnp.exp((idx + 1.0)[:, None] * log_gamma), C_HEAD_QK, axis=1)
    chunk_decay = jnp.broadcast_to(jnp.repeat(jnp.exp(CHUNK * log_gamma), C_HEAD_QK)[:, None],
                                   (C_QK, LANES))
    return cos, sin, intra, k_decay, q_decay, chunk_decay


def _retention_body(qk_ref, v_ref, g_ref, cos_ref, sin_ref, intra_ref, kdec_ref, qdec_ref, cdec_ref, o_ref,
                    state_ref):
    c = pl.program_id(1)

    @pl.when(c == 0)
    def _():
        state_ref[...] = jnp.zeros_like(state_ref)

    lane = lax.broadcasted_iota(jnp.int32, (CHUNK, LANES), 1)
    first_half = (lane % C_HEAD_QK) < (C_HEAD_QK // 2)
    low_head = lane < C_HEAD_QK
    low_rows = lax.broadcasted_iota(jnp.int32, (LANES, LANES), 0) < C_HEAD_QK
    cos = cos_ref[...]
    sin = sin_ref[...]

    def rotary(x):
        partner = jnp.where(first_half, pltpu.roll(x, LANES - C_HEAD_QK // 2, 1), pltpu.roll(x, C_HEAD_QK // 2, 1))
        return x * cos + partner * sin

    for pair in range(C_HEADS // 2):
        pc = slice(pair * LANES, (pair + 1) * LANES)
        q2 = rotary(qk_ref[:, pc])
        k2 = rotary(qk_ref[:, C_QK + pair * LANES:C_QK + (pair + 1) * LANES]) * (C_HEAD_QK ** -0.5)
        qd2 = q2 * qdec_ref[:, pc]
        kd2_t = (k2 * kdec_ref[:, pc]).T
        prev = state_ref[pc, :]
        new_state = []
        for half in range(2):
            h = 2 * pair + half
            hc = slice(h * LANES, (h + 1) * LANES)
            sel = low_head if half == 0 else ~low_head
            scores = _dot_nt(jnp.where(sel, q2, 0.0), k2) * intra_ref[h]
            vh = v_ref[:, hc]
            y = _dot(scores, vh) + _dot(jnp.where(sel, qd2, 0.0), prev)
            mu = jnp.mean(y, axis=-1, keepdims=True)
            var = jnp.mean(jnp.square(y - mu), axis=-1, keepdims=True)
            y = (y - mu) * lax.rsqrt(var + EPS)
            o_ref[:, hc] = (_silu(g_ref[:, hc]) * y).astype(o_ref.dtype)
            new_state.append(_dot(kd2_t, vh))
        state_ref[pc, :] = prev * cdec_ref[pc, :] + jnp.where(low_rows, new_state[0], new_state[1])


def retention(p_cd, tables, bsz, seq):
    nc = seq // CHUNK
    m = bsz * seq
    cos, sin, intra, k_decay, q_decay, chunk_decay = tables
    row = lambda b, c: b * nc + c
    const2 = lambda b, c: (0, 0)
    return pl.pallas_call(
        _retention_body,
        out_shape=jax.ShapeDtypeStruct((m, WIDTH), BF16),
        grid=(bsz, nc),
        in_specs=[
            pl.BlockSpec((CHUNK, 2 * C_QK), lambda b, c: (row(b, c), 0)),
            pl.BlockSpec((CHUNK, WIDTH), lambda b, c: (row(b, c), 1)),
            pl.BlockSpec((CHUNK, WIDTH), lambda b, c: (row(b, c), 2)),
            pl.BlockSpec((CHUNK, LANES), lambda b, c: (c, 0)),
            pl.BlockSpec((CHUNK, LANES), lambda b, c: (c, 0)),
            pl.BlockSpec((C_HEADS, CHUNK, CHUNK), lambda b, c: (0, 0, 0)),
            pl.BlockSpec((CHUNK, C_QK), const2),
            pl.BlockSpec((CHUNK, C_QK), const2),
            pl.BlockSpec((C_QK, LANES), const2),
        ],
        out_specs=pl.BlockSpec((CHUNK, WIDTH), lambda b, c: (row(b, c), 0)),
        scratch_shapes=[pltpu.VMEM((C_QK, LANES), F32)],
        compiler_params=_params("parallel", "arbitrary"),
        name="retention",
    )(p_cd, p_cd, p_cd, cos, sin, intra, k_decay, q_decay, chunk_decay)


def _forget_cumsum_body(small_ref, fb_ref, cum_ref, cum_t_ref, carry_ref):
    c = pl.program_id(1)

    @pl.when(c == 0)
    def _():
        carry_ref[...] = jnp.zeros_like(carry_ref)

    log_f = -_softplus(-(small_ref[...] + fb_ref[...]))
    cum = _dot_f32(_lower_tri(CHUNK).astype(F32), log_f) + carry_ref[...]
    carry_ref[...] = cum[CHUNK - 1:CHUNK, :]
    cum_ref[...] = cum
    cum_t_ref[...] = cum.T


def forget_cumsum(p_small, fb_row, bsz, seq):
    nc = seq // CHUNK
    return pl.pallas_call(
        _forget_cumsum_body,
        out_shape=[jax.ShapeDtypeStruct((bsz * seq, LANES), F32), jax.ShapeDtypeStruct((bsz, LANES, seq), F32)],
        grid=(bsz, nc),
        in_specs=[pl.BlockSpec((CHUNK, LANES), lambda b, c: (b * nc + c, 0)),
                  pl.BlockSpec((1, LANES), lambda b, c: (0, 0))],
        out_specs=[pl.BlockSpec((CHUNK, LANES), lambda b, c: (b * nc + c, 0)),
                   pl.BlockSpec((None, LANES, CHUNK), lambda b, c: (b, 0, c))],
        scratch_shapes=[pltpu.VMEM((1, LANES), F32)],
        compiler_params=_params("parallel", "arbitrary"),
        name="forget_cumsum",
    )(p_small, fb_row)


def _fox_body(q_ref, k_ref, v_ref, cum_ref, cum_t_ref, o_ref, *, blk):
    h = pl.program_id(1)
    qi = pl.program_id(2)
    q = q_ref[...].astype(BF16)
    lane = lax.broadcasted_iota(jnp.int32, (blk, LANES), 1)
    cq = jnp.sum(jnp.where(lane == SMALL_FL + h, cum_ref[...], 0.0), axis=-1, keepdims=True)
    scale = D_HEAD_DIM ** -0.5

    def scores(kj):
        ks = pl.multiple_of(kj * blk, blk)
        s = _dot_nt(q, k_ref[pl.ds(ks, blk), :]) * scale
        return s + cq - cum_t_ref[pl.ds(SMALL_FL + h, 1), pl.ds(ks, blk)], ks

    def update(carry, s, ks):
        m_prev, l_prev, acc = carry
        m_new = jnp.maximum(m_prev, jnp.max(s, axis=-1, keepdims=True))
        alpha = jnp.exp(m_prev - m_new)
        p = jnp.exp(s - m_new)
        l_new = alpha * l_prev + jnp.sum(p, axis=-1, keepdims=True)
        acc = alpha * acc + _dot(p, v_ref[pl.ds(ks, blk), :])
        return m_new, l_new, acc

    def full_block(kj, carry):
        s, ks = scores(kj)
        return update(carry, s, ks)

    init = (jnp.full((blk, 1), -jnp.inf, F32), jnp.zeros((blk, 1), F32), jnp.zeros((blk, D_HEAD_DIM), F32))
    carry = lax.fori_loop(0, qi, full_block, init)
    s, ks = scores(qi)
    s = jnp.where(_lower_tri(blk), s, -jnp.inf)
    _, l_fin, acc = update(carry, s, ks)
    o_ref[...] = (acc / l_fin).astype(o_ref.dtype)


def fox(p_cd, cum, cum_t, bsz, seq, blk=512):
    nq = seq // blk
    col0 = (2 * C_QK + 2 * WIDTH) // D_HEAD_DIM
    nh = D_HEADS
    return pl.pallas_call(
        functools.partial(_fox_body, blk=blk),
        out_shape=jax.ShapeDtypeStruct((bsz * seq, WIDTH), BF16),
        grid=(bsz, nh, nq),
        in_specs=[
            pl.BlockSpec((blk, D_HEAD_DIM), lambda b, h, i: (b * nq + i, col0 + h)),
            pl.BlockSpec((seq, D_HEAD_DIM), lambda b, h, i: (b, col0 + nh + h)),
            pl.BlockSpec((seq, D_HEAD_DIM), lambda b, h, i: (b, col0 + 2 * nh + h)),
            pl.BlockSpec((blk, LANES), lambda b, h, i: (b * nq + i, 0)),
            pl.BlockSpec((None, 8, seq), lambda b, h, i: (b, SMALL_FL // 8, 0)),
        ],
        out_specs=pl.BlockSpec((blk, D_HEAD_DIM), lambda b, h, i: (b * nq + i, h)),
        compiler_params=_params("parallel", "parallel", "arbitrary"),
        name="fox",
    )(p_cd, p_cd, p_cd, cum, cum_t)


def _merge_body(ya_ref, yb_ref, yc_ref, yd_ref, ga_ref, gb_ref, gc_ref, gd_ref, wb_ref, wo_ref, x_ref, gn_ref,
                xo_ref, ho_ref, acc_ref, *, nj):
    j = pl.program_id(1)

    @pl.when(j == 0)
    def _():
        acc_ref[...] = jnp.zeros_like(acc_ref)

    merged = ga_ref[...] * _dot(ya_ref[...], wb_ref[0])
    merged = merged + gb_ref[...] * _dot(yb_ref[...], wb_ref[1])
    merged = merged + gc_ref[...] * _dot(yc_ref[...], wb_ref[2])
    merged = merged + gd_ref[...] * _dot(yd_ref[...], wb_ref[3])
    acc_ref[...] += _dot(merged, wo_ref[...])

    @pl.when(j == nj - 1)
    def _():
        x_new = x_ref[...] + acc_ref[...]
        xo_ref[...] = x_new
        ho_ref[...] = _rmsnorm_rows(x_new, gn_ref[...]).astype(ho_ref.dtype)


def merge(ys, gates, w_branch, w_mix_out, x, g_next, layer, tm=512, tn=256):
    m, d = x.shape
    nj = d // tn
    y_spec = pl.BlockSpec((tm, WIDTH), lambda i, j: (i, 0))
    gate_specs = [pl.BlockSpec((tm, tn), functools.partial(lambda i, j, n: (i, n * nj + j), n=n)) for n in range(4)]
    row_spec = pl.BlockSpec((tm, d), lambda i, j: (i, 0))
    return pl.pallas_call(
        functools.partial(_merge_body, nj=nj),
        out_shape=[jax.ShapeDtypeStruct((m, d), F32), jax.ShapeDtypeStruct((m, d), BF16)],
        grid=(m // tm, nj),
        in_specs=[y_spec] * 4 + gate_specs + [
            pl.BlockSpec((None, 4, WIDTH, tn), lambda i, j: (layer, 0, 0, j)),
            pl.BlockSpec((None, tn, d), lambda i, j: (layer, j, 0)),
            row_spec,
            pl.BlockSpec((1, d), lambda i, j: (0, 0)),
        ],
        out_specs=[row_spec, row_spec],
        scratch_shapes=[pltpu.VMEM((tm, d), F32)],
        compiler_params=_params("parallel", "arbitrary"),
        name="merge",
    )(*ys, gates, gates, gates, gates, w_branch, w_mix_out, x, g_next.reshape(1, d))


def _pad_lanes(v, offset):
    return jnp.zeros((1, LANES), F32).at[0, offset:offset + v.shape[0]].set(v.astype(F32))


def kernel(x, ffn1_norm, ffn1_w_in, ffn1_w_out, mix_norm, w_mix_in, sgu_norm, sgu_w, sgu_b, conv_w, conv_b,
           dt_bias, a_log, d_skip, ssm_norm, forget_bias, w_branch, w_mix_out, ffn2_norm, ffn2_w_in, ffn2_w_out,
           final_norm):
    bsz, seq, d = x.shape
    depth = ffn1_norm.shape[0]
    m = bsz * seq
    x = x.reshape(m, d)
    tables = _retention_tables(seq)
    h = rmsnorm(x, ffn1_norm[0], BF16)
    for l in range(depth):
        x, h = ffn(h, x, ffn1_w_in, ffn1_w_out, l, mix_norm[l], emit_x=True, h_dtype=BF16)

        w_l = w_mix_in[l]
        w_small = jnp.concatenate(
            [w_l[:, COL_FL:COL_G], w_l[:, COL_DT:COL_C], jnp.zeros((d, LANES - D_HEADS - B_HEADS), F32)], axis=1)
        p_ab = matmul(h, w_mix_in, N_AB, layer=l)
        p_small = matmul(h, w_small, LANES)
        p_cd = matmul(h, w_l[:, COL_C:COL_FL].astype(BF16), N_CD)
        gates = matmul(h, w_l[:, COL_G:].astype(BF16), 4 * d, act="sigmoid")

        b_full = jnp.repeat(sgu_b[l].T, LANES, axis=1)
        y_a = sgu(p_ab, sgu_norm[l], sgu_w, b_full, l)
        y_b = ssd(p_ab, p_small, conv_w, conv_b, _pad_lanes(dt_bias[l], SMALL_DT), _pad_lanes(a_log[l], SMALL_DT),
                  jnp.repeat(d_skip[l], B_HEAD_DIM).reshape(1, WIDTH), ssm_norm[l], l, bsz, seq)
        y_c = retention(p_cd, tables, bsz, seq)
        cum, cum_t = forget_cumsum(p_small, _pad_lanes(forget_bias[l], SMALL_FL), bsz, seq)
        y_d = fox(p_cd, cum, cum_t, bsz, seq)

        x, h = merge((y_a, y_b, y_c, y_d), gates, w_branch, w_mix_out, x, ffn2_norm[l], l)
        if l + 1 < depth:
            x, h = ffn(h, x, ffn2_w_in, ffn2_w_out, l, ffn1_norm[l + 1], emit_x=True, h_dtype=BF16)
        else:
            h = ffn(h, x, ffn2_w_in, ffn2_w_out, l, final_norm, emit_x=False, h_dtype=F32)
    return h.reshape(bsz, seq, d)
```

```python
import functools
import math

import jax
import jax.numpy as jnp
from jax import lax
from jax.experimental import pallas as pl
from jax.experimental.pallas import tpu as pltpu

F32 = jnp.float32
BF16 = jnp.bfloat16
HIGHEST = lax.Precision.HIGHEST

EPS = 1e-6
CHUNK = 128
LANES = 128
VMEM_LIMIT = 56 * 1024 * 1024

D_MODEL = 2048
D_FF = 5632
WIDTH = 1024
A_GROUPS = 8
B_HEADS = 16
B_HEAD_DIM = 64
B_STATE = 128
B_GROUPS = 2
B_CONV = 4
B_CONV_DIM = WIDTH + 2 * B_GROUPS * B_STATE
C_HEADS = 8
C_HEAD_QK = 64
C_QK = C_HEADS * C_HEAD_QK
D_HEADS = 8
D_HEAD_DIM = 128
ROPE_BASE = 10000.0

COL_DT = 2 * WIDTH + WIDTH + B_CONV_DIM
COL_C = COL_DT + B_HEADS
COL_FL = COL_C + 2 * C_QK + 2 * WIDTH + 3 * WIDTH
COL_G = COL_FL + D_HEADS
N_AB = COL_DT
N_CD = COL_FL - COL_C
SMALL_FL = 0
SMALL_DT = D_HEADS


def _params(*sem):
    return pltpu.CompilerParams(dimension_semantics=sem, vmem_limit_bytes=VMEM_LIMIT)


def _dot(a, b):
    return jnp.dot(a.astype(BF16), b.astype(BF16), preferred_element_type=F32)


def _dot_nt(a, b):
    return lax.dot_general(a.astype(BF16), b.astype(BF16), (((1,), (1,)), ((), ())),
                           preferred_element_type=F32)


def _dot_f32(a, b):
    return jnp.dot(a, b, precision=HIGHEST, preferred_element_type=F32)


def _silu(x):
    return x * jax.nn.sigmoid(x)


def _softplus(x):
    return jnp.maximum(x, 0.0) + jnp.log1p(jnp.exp(-jnp.abs(x)))


def _lower_tri(n):
    return lax.broadcasted_iota(jnp.int32, (n, n), 1) <= lax.broadcasted_iota(jnp.int32, (n, n), 0)


def _rmsnorm_rows(x, g):
    return x * lax.rsqrt(jnp.mean(x * x, axis=-1, keepdims=True) + EPS) * g


def _rmsnorm_body(x_ref, g_ref, o_ref):
    o_ref[...] = _rmsnorm_rows(x_ref[...], g_ref[...]).astype(o_ref.dtype)


def rmsnorm(x, g, out_dtype, tm=512):
    m, d = x.shape
    return pl.pallas_call(
        _rmsnorm_body,
        out_shape=jax.ShapeDtypeStruct((m, d), out_dtype),
        grid=(m // tm,),
        in_specs=[pl.BlockSpec((tm, d), lambda i: (i, 0)), pl.BlockSpec((1, d), lambda i: (0, 0))],
        out_specs=pl.BlockSpec((tm, d), lambda i: (i, 0)),
        compiler_params=_params("parallel"),
        name="rmsnorm",
    )(x, g.reshape(1, d))


def _load_residual_rows(x_hbm, acc_ref, tm):
    rows = pl.ds(pl.multiple_of(pl.program_id(0) * tm, tm), tm)
    pltpu.sync_copy(x_hbm.at[rows, :], acc_ref)


def _ffn_body(h_ref, x_hbm, gn_ref, wg_ref, wu_ref, wo_ref, *refs, nf, tm, emit_x):
    xo_ref, ho_ref = refs if emit_x else (refs[1], refs[0])
    f = pl.program_id(1)

    @pl.when(f == 0)
    def _():
        _load_residual_rows(x_hbm, xo_ref, tm)

    h = h_ref[...]
    gate = _dot(h, wg_ref[...])
    up = _dot(h, wu_ref[...])
    xo_ref[...] += _dot(0.5 * _silu(gate) * up, wo_ref[...])

    @pl.when(f == nf - 1)
    def _():
        ho_ref[...] = _rmsnorm_rows(xo_ref[...], gn_ref[...]).astype(ho_ref.dtype)


def ffn(h, x, w_in, w_out, layer, g_next, *, emit_x, h_dtype, tm=1024, tf=256):
    m, d = x.shape
    nf = D_FF // tf
    row_spec = pl.BlockSpec((tm, d), lambda i, f: (i, 0))
    out_shape = [jax.ShapeDtypeStruct((m, d), h_dtype)]
    out_specs = [row_spec]
    scratch = [pltpu.VMEM((tm, d), F32)]
    if emit_x:
        out_shape.insert(0, jax.ShapeDtypeStruct((m, d), F32))
        out_specs.insert(0, row_spec)
        scratch = []
    res = pl.pallas_call(
        functools.partial(_ffn_body, nf=nf, tm=tm, emit_x=emit_x),
        out_shape=out_shape,
        grid=(m // tm, nf),
        in_specs=[
            row_spec,
            pl.BlockSpec(memory_space=pl.ANY),
            pl.BlockSpec((1, d), lambda i, f: (0, 0)),
            pl.BlockSpec((None, d, tf), lambda i, f: (layer, 0, f)),
            pl.BlockSpec((None, d, tf), lambda i, f: (layer, 0, nf + f)),
            pl.BlockSpec((None, tf, d), lambda i, f: (layer, f, 0)),
        ],
        out_specs=out_specs,
        scratch_shapes=scratch,
        compiler_params=_params("parallel", "arbitrary"),
        name="ffn",
    )(h, x, g_next.reshape(1, d), w_in, w_in, w_out)
    return res if emit_x else res[0]


def _realign_body(main_ref, tail_ref, o_ref, *, off, tn):
    tk = main_ref.shape[0]
    keep = lax.broadcasted_iota(jnp.int32, (tk, LANES), 1) < LANES - off
    nslab = tn // LANES
    rolled = [pltpu.roll(main_ref[:, s * LANES:(s + 1) * LANES], LANES - off, 1) for s in range(nslab)]
    rolled.append(pltpu.roll(tail_ref[...], LANES - off, 1))
    for s in range(nslab):
        o_ref[:, s * LANES:(s + 1) * LANES] = jnp.where(keep, rolled[s], rolled[s + 1]).astype(o_ref.dtype)


def realign(w, layer, first_col, n, tk=512, tn=512):
    k = w.shape[1]
    off = first_col % LANES
    base = first_col - off
    assert off and base % tn == 0 and n % tn == 0 and k % tk == 0
    return pl.pallas_call(
        functools.partial(_realign_body, off=off, tn=tn),
        out_shape=jax.ShapeDtypeStruct((k, n), BF16),
        grid=(k // tk, n // tn),
        in_specs=[
            pl.BlockSpec((None, tk, tn), lambda r, j: (layer, r, base // tn + j)),
            pl.BlockSpec((None, tk, LANES), lambda r, j: (layer, r, (base + (j + 1) * tn) // LANES)),
        ],
        out_specs=pl.BlockSpec((tk, tn), lambda r, j: (r, j)),
        compiler_params=_params("parallel", "parallel"),
        name="realign",
    )(w, w)


def _matmul_body(h_ref, w_ref, o_ref, *, act):
    acc = _dot(h_ref[...], w_ref[...])
    if act == "sigmoid":
        acc = jax.nn.sigmoid(acc)
    o_ref[...] = acc.astype(o_ref.dtype)


def matmul(h, w, n, *, layer=None, col_block=0, act=None, out_dtype=F32, tm=2048, tn=512):
    m, k = h.shape
    tn = min(tn, n)
    if layer is None:
        w_spec = pl.BlockSpec((k, tn), lambda i, j: (0, col_block + j))
    else:
        w_spec = pl.BlockSpec((None, k, tn), lambda i, j: (layer, 0, col_block + j))
    return pl.pallas_call(
        functools.partial(_matmul_body, act=act),
        out_shape=jax.ShapeDtypeStruct((m, n), out_dtype),
        grid=(m // tm, n // tn),
        in_specs=[pl.BlockSpec((tm, k), lambda i, j: (i, 0)), w_spec],
        out_specs=pl.BlockSpec((tm, tn), lambda i, j: (i, j)),
        compiler_params=_params("parallel", "arbitrary"),
        name="proj",
    )(h, w)


def _sgu_body(p_ref, lng_ref, w_ref, b_ref, o_ref):
    p = jax.nn.gelu(p_ref[...].astype(F32))
    u = p[:, :WIDTH]
    v = p[:, WIDTH:]
    mu = jnp.mean(v, axis=-1, keepdims=True)
    var = jnp.mean(jnp.square(v - mu), axis=-1, keepdims=True)
    vn = (v - mu) * lax.rsqrt(var + EPS) * lng_ref[...]
    causal = _lower_tri(CHUNK)
    for g in range(A_GROUPS):
        cols = slice(g * LANES, (g + 1) * LANES)
        w = jnp.where(causal, w_ref[g], 0.0)
        mixed = _dot(w, vn[:, cols]) + b_ref[:, cols]
        o_ref[:, cols] = (u[:, cols] * mixed).astype(o_ref.dtype)


def sgu(p_ab, ln_g, w_s, b_full, layer):
    m = p_ab.shape[0]
    return pl.pallas_call(
        _sgu_body,
        out_shape=jax.ShapeDtypeStruct((m, WIDTH), BF16),
        grid=(m // CHUNK,),
        in_specs=[
            pl.BlockSpec((CHUNK, 2 * WIDTH), lambda c: (c, 0)),
            pl.BlockSpec((1, WIDTH), lambda c: (0, 0)),
            pl.BlockSpec((None, A_GROUPS, CHUNK, CHUNK), lambda c: (layer, 0, 0, 0)),
            pl.BlockSpec((CHUNK, WIDTH), lambda c: (0, 0)),
        ],
        out_specs=pl.BlockSpec((CHUNK, WIDTH), lambda c: (c, 0)),
        compiler_params=_params("parallel"),
        name="sgu",
    )(p_ab, ln_g.reshape(1, WIDTH), w_s, b_full)


def _ssd_body(z_ref, xbc_ref, small_ref, cw_ref, cb_ref, dtb_ref, alog_ref, dsk_ref, ng_ref, o_ref,
              xpad_ref, state_ref, y_ref):
    c = pl.program_id(1)

    @pl.when(c == 0)
    def _():
        xpad_ref[0:8, :] = jnp.zeros((8, B_CONV_DIM), F32)
        state_ref[...] = jnp.zeros_like(state_ref)

    xpad_ref[8:8 + CHUNK, :] = xbc_ref[...].astype(F32)
    conv = cb_ref[...]
    for j in range(B_CONV):
        conv = conv + cw_ref[j:j + 1, :] * xpad_ref[pl.ds(8 - (B_CONV - 1) + j, CHUNK), :]
    xpad_ref[0:8, :] = xpad_ref[CHUNK:CHUNK + 8, :]
    xbc = _silu(conv)
    xs = xbc[:, :WIDTH]
    bm = xbc[:, WIDTH:WIDTH + B_GROUPS * B_STATE]
    cm = xbc[:, WIDTH + B_GROUPS * B_STATE:]

    dt = _softplus(small_ref[...] + dtb_ref[...])
    a = -jnp.exp(alog_ref[...])
    causal = _lower_tri(CHUNK)
    acum = _dot_f32(causal.astype(F32), dt * a)
    acum_t = acum.T
    expand = (lax.broadcasted_iota(jnp.int32, (LANES, WIDTH), 0) - SMALL_DT
              == lax.broadcasted_iota(jnp.int32, (LANES, WIDTH), 1) // B_HEAD_DIM).astype(F32)
    dt_e = _dot_f32(dt, expand)
    eacum_e = _dot_f32(jnp.exp(acum), expand)
    dend_e = _dot_f32(jnp.exp(acum[CHUNK - 1:CHUNK, :] - acum), expand)
    xdt = xs * dt_e
    xend = xdt * dend_e
    lane = lax.broadcasted_iota(jnp.int32, (CHUNK, LANES), 1)
    low_half = lane < B_HEAD_DIM
    hpg = B_HEADS // B_GROUPS
    gw = WIDTH // B_GROUPS

    for g in range(B_GROUPS):
        bg = bm[:, g * B_STATE:(g + 1) * B_STATE]
        cg = cm[:, g * B_STATE:(g + 1) * B_STATE]
        cb = _dot_nt(cg, bg)
        gcols = slice(g * gw, (g + 1) * gw)
        prev = state_ref[:, gcols]
        y_off = _dot(cg, prev) * eacum_e[:, gcols]
        for pair in range(hpg // 2):
            pcols = slice(g * gw + pair * LANES, g * gw + (pair + 1) * LANES)
            x_pair = xdt[:, pcols]
            y_pair = jnp.zeros((CHUNK, LANES), F32)
            for half in range(2):
                hl = SMALL_DT + g * hpg + 2 * pair + half
                seg = acum[:, hl:hl + 1] - acum_t[hl:hl + 1, :]
                decay = jnp.exp(jnp.where(causal, seg, -jnp.inf))
                x_half = jnp.where(low_half if half == 0 else ~low_half, x_pair, 0.0)
                y_pair = y_pair + _dot(cb * decay, x_half)
            y_pair = y_pair + y_off[:, pair * LANES:(pair + 1) * LANES] + xs[:, pcols] * dsk_ref[:, pcols]
            y_ref[:, pcols] = y_pair
        st = _dot(bg.T, xend[:, gcols])
        state_ref[:, gcols] = prev * eacum_e[CHUNK - 1:CHUNK, gcols] + st

    y = y_ref[...] * _silu(z_ref[...].astype(F32))
    for g in range(B_GROUPS):
        gcols = slice(g * gw, (g + 1) * gw)
        yg = y[:, gcols]
        yg = yg * lax.rsqrt(jnp.mean(yg * yg, axis=-1, keepdims=True) + EPS)
        o_ref[:, gcols] = (yg * ng_ref[:, gcols]).astype(o_ref.dtype)


def ssd(p_ab, p_small, conv_w, conv_b, dtb_row, alog_row, dskip_row, norm_g, layer, bsz, seq):
    nc = seq // CHUNK
    m = bsz * seq
    row = lambda b, c: b * nc + c
    return pl.pallas_call(
        _ssd_body,
        out_shape=jax.ShapeDtypeStruct((m, WIDTH), BF16),
        grid=(bsz, nc),
        in_specs=[
            pl.BlockSpec((CHUNK, WIDTH), lambda b, c: (row(b, c), 2)),
            pl.BlockSpec((CHUNK, B_CONV_DIM), lambda b, c: (row(b, c), 2)),
            pl.BlockSpec((CHUNK, LANES), lambda b, c: (row(b, c), 0)),
            pl.BlockSpec((None, B_CONV, B_CONV_DIM), lambda b, c: (layer, 0, 0)),
            pl.BlockSpec((None, 1, B_CONV_DIM), lambda b, c: (layer, 0, 0)),
            pl.BlockSpec((1, LANES), lambda b, c: (0, 0)),
            pl.BlockSpec((1, LANES), lambda b, c: (0, 0)),
            pl.BlockSpec((1, WIDTH), lambda b, c: (0, 0)),
            pl.BlockSpec((1, WIDTH), lambda b, c: (0, 0)),
        ],
        out_specs=pl.BlockSpec((CHUNK, WIDTH), lambda b, c: (row(b, c), 0)),
        scratch_shapes=[pltpu.VMEM((CHUNK + 8, B_CONV_DIM), F32), pltpu.VMEM((B_STATE, WIDTH), F32),
                        pltpu.VMEM((CHUNK, WIDTH), F32)],
        compiler_params=_params("parallel", "arbitrary"),
        name="ssd",
    )(p_ab, p_ab, p_small, conv_w, conv_b.reshape(conv_b.shape[0], 1, B_CONV_DIM), dtb_row, alog_row,
      dskip_row, norm_g.reshape(1, WIDTH))


def _retention_tables(seq):
    half = C_HEAD_QK // 2
    inv_freq = 1.0 / (ROPE_BASE ** (jnp.arange(half, dtype=F32) / half))
    ang = jnp.arange(seq, dtype=F32)[:, None] * inv_freq[None, :]
    cos = jnp.tile(jnp.cos(ang), (1, 2 * LANES // C_HEAD_QK))
    sin = jnp.tile(jnp.concatenate([-jnp.sin(ang), jnp.sin(ang)], axis=1), (1, LANES // C_HEAD_QK))
    log_gamma = jnp.log(1.0 - 2.0 ** (-5.0 - jnp.arange(C_HEADS, dtype=F32)))
    idx = jnp.arange(CHUNK, dtype=F32)
    mask = idx[:, None] >= idx[None, :]
    rel = jnp.where(mask, idx[:, None] - idx[None, :], 0.0)
    intra = (jnp.exp(rel[..., None] * log_gamma) * mask[..., None]).transpose(2, 0, 1)
    k_decay = jnp.repeat(jnp.exp((CHUNK - 1.0 - idx)[:, None] * log_gamma), C_HEAD_QK, axis=1)
    q_decay = jnp.repeat(jnp.exp((idx + 1.0)[:, None] * log_gamma), C_HEAD_QK, axis=1)
    chunk_decay = jnp.broadcast_to(jnp.repeat(jnp.exp(CHUNK * log_gamma), C_HEAD_QK)[:, None],
                                   (C_QK, LANES))
    return cos, sin, intra, k_decay, q_decay, chunk_decay


def _retention_body(qk_ref, v_ref, g_ref, cos_ref, sin_ref, intra_ref, kdec_ref, qdec_ref, cdec_ref, o_ref,
                    state_ref):
    c = pl.program_id(1)

    @pl.when(c == 0)
    def _():
        state_ref[...] = jnp.zeros_like(state_ref)

    lane = lax.broadcasted_iota(jnp.int32, (CHUNK, LANES), 1)
    first_half = (lane % C_HEAD_QK) < (C_HEAD_QK // 2)
    low_head = lane < C_HEAD_QK
    low_rows = lax.broadcasted_iota(jnp.int32, (LANES, LANES), 0) < C_HEAD_QK
    cos = cos_ref[...]
    sin = sin_ref[...]

    def rotary(x):
        partner = jnp.where(first_half, pltpu.roll(x, LANES - C_HEAD_QK // 2, 1), pltpu.roll(x, C_HEAD_QK // 2, 1))
        return x * cos + partner * sin

    for pair in range(C_HEADS // 2):
        pc = slice(pair * LANES, (pair + 1) * LANES)
        q2 = rotary(qk_ref[:, pc].astype(F32))
        k2 = rotary(qk_ref[:, C_QK + pair * LANES:C_QK + (pair + 1) * LANES].astype(F32)) * (C_HEAD_QK ** -0.5)
        qd2 = q2 * qdec_ref[:, pc]
        kd2_t = (k2 * kdec_ref[:, pc]).T
        prev = state_ref[pc, :]
        new_state = []
        for half in range(2):
            h = 2 * pair + half
            hc = slice(h * LANES, (h + 1) * LANES)
            sel = low_head if half == 0 else ~low_head
            scores = _dot_nt(jnp.where(sel, q2, 0.0), k2) * intra_ref[h]
            vh = v_ref[:, hc]
            y = _dot(scores, vh) + _dot(jnp.where(sel, qd2, 0.0), prev)
            mu = jnp.mean(y, axis=-1, keepdims=True)
            var = jnp.mean(jnp.square(y - mu), axis=-1, keepdims=True)
            y = (y - mu) * lax.rsqrt(var + EPS)
            o_ref[:, hc] = (_silu(g_ref[:, hc].astype(F32)) * y).astype(o_ref.dtype)
            new_state.append(_dot(kd2_t, vh))
        state_ref[pc, :] = prev * cdec_ref[pc, :] + jnp.where(low_rows, new_state[0], new_state[1])


def retention(p_cd, tables, bsz, seq):
    nc = seq // CHUNK
    m = bsz * seq
    cos, sin, intra, k_decay, q_decay, chunk_decay = tables
    row = lambda b, c: b * nc + c
    const2 = lambda b, c: (0, 0)
    return pl.pallas_call(
        _retention_body,
        out_shape=jax.ShapeDtypeStruct((m, WIDTH), BF16),
        grid=(bsz, nc),
        in_specs=[
            pl.BlockSpec((CHUNK, 2 * C_QK), lambda b, c: (row(b, c), 0)),
            pl.BlockSpec((CHUNK, WIDTH), lambda b, c: (row(b, c), 1)),
            pl.BlockSpec((CHUNK, WIDTH), lambda b, c: (row(b, c), 2)),
            pl.BlockSpec((CHUNK, LANES), lambda b, c: (c, 0)),
            pl.BlockSpec((CHUNK, LANES), lambda b, c: (c, 0)),
            pl.BlockSpec((C_HEADS, CHUNK, CHUNK), lambda b, c: (0, 0, 0)),
            pl.BlockSpec((CHUNK, C_QK), const2),
            pl.BlockSpec((CHUNK, C_QK), const2),
            pl.BlockSpec((C_QK, LANES), const2),
        ],
        out_specs=pl.BlockSpec((CHUNK, WIDTH), lambda b, c: (row(b, c), 0)),
        scratch_shapes=[pltpu.VMEM((C_QK, LANES), F32)],
        compiler_params=_params("parallel", "arbitrary"),
        name="retention",
    )(p_cd, p_cd, p_cd, cos, sin, intra, k_decay, q_decay, chunk_decay)


def _forget_cumsum_body(small_ref, fb_ref, cum_ref, cum_t_ref, carry_ref):
    c = pl.program_id(1)

    @pl.when(c == 0)
    def _():
        carry_ref[...] = jnp.zeros_like(carry_ref)

    log_f = -_softplus(-(small_ref[...] + fb_ref[...]))
    cum = _dot_f32(_lower_tri(CHUNK).astype(F32), log_f) + carry_ref[...]
    carry_ref[...] = cum[CHUNK - 1:CHUNK, :]
    cum_ref[...] = cum
    cum_t_ref[...] = cum.T


def forget_cumsum(p_small, fb_row, bsz, seq):
    nc = seq // CHUNK
    return pl.pallas_call(
        _forget_cumsum_body,
        out_shape=[jax.ShapeDtypeStruct((bsz * seq, LANES), F32), jax.ShapeDtypeStruct((bsz, LANES, seq), F32)],
        grid=(bsz, nc),
        in_specs=[pl.BlockSpec((CHUNK, LANES), lambda b, c: (b * nc + c, 0)),
                  pl.BlockSpec((1, LANES), lambda b, c: (0, 0))],
        out_specs=[pl.BlockSpec((CHUNK, LANES), lambda b, c: (b * nc + c, 0)),
                   pl.BlockSpec((None, LANES, CHUNK), lambda b, c: (b, 0, c))],
        scratch_shapes=[pltpu.VMEM((1, LANES), F32)],
        compiler_params=_params("parallel", "arbitrary"),
        name="forget_cumsum",
    )(p_small, fb_row)


def _fox_body(q_ref, k_ref, v_ref, cum_ref, cum_t_ref, o_ref, *, blk):
    h = pl.program_id(1)
    qi = pl.program_id(2)
    q = q_ref[...].astype(BF16)
    lane = lax.broadcasted_iota(jnp.int32, (blk, LANES), 1)
    cq = jnp.sum(jnp.where(lane == SMALL_FL + h, cum_ref[...], 0.0), axis=-1, keepdims=True)
    scale = D_HEAD_DIM ** -0.5

    def scores(kj):
        ks = pl.multiple_of(kj * blk, blk)
        s = _dot_nt(q, k_ref[pl.ds(ks, blk), :]) * scale
        return s + cq - cum_t_ref[pl.ds(SMALL_FL + h, 1), pl.ds(ks, blk)], ks

    def update(carry, s, ks):
        m_prev, l_prev, acc = carry
        m_new = jnp.maximum(m_prev, jnp.max(s, axis=-1, keepdims=True))
        alpha = jnp.exp(m_prev - m_new)
        p = jnp.exp(s - m_new)
        l_new = alpha * l_prev + jnp.sum(p, axis=-1, keepdims=True)
        acc = alpha * acc + _dot(p, v_ref[pl.ds(ks, blk), :])
        return m_new, l_new, acc

    def full_block(kj, carry):
        s, ks = scores(kj)
        return update(carry, s, ks)

    init = (jnp.full((blk, 1), -jnp.inf, F32), jnp.zeros((blk, 1), F32), jnp.zeros((blk, D_HEAD_DIM), F32))
    carry = lax.fori_loop(0, qi, full_block, init)
    s, ks = scores(qi)
    s = jnp.where(_lower_tri(blk), s, -jnp.inf)
    _, l_fin, acc = update(carry, s, ks)
    o_ref[...] = (acc / l_fin).astype(o_ref.dtype)


def fox(p_cd, cum, cum_t, bsz, seq, blk=512):
    nq = seq // blk
    col0 = (2 * C_QK + 2 * WIDTH) // D_HEAD_DIM
    nh = D_HEADS
    return pl.pallas_call(
        functools.partial(_fox_body, blk=blk),
        out_shape=jax.ShapeDtypeStruct((bsz * seq, WIDTH), BF16),
        grid=(bsz, nh, nq),
        in_specs=[
            pl.BlockSpec((blk, D_HEAD_DIM), lambda b, h, i: (b * nq + i, col0 + h)),
            pl.BlockSpec((seq, D_HEAD_DIM), lambda b, h, i: (b, col0 + nh + h)),
            pl.BlockSpec((seq, D_HEAD_DIM), lambda b, h, i: (b, col0 + 2 * nh + h)),
            pl.BlockSpec((blk, LANES), lambda b, h, i: (b * nq + i, 0)),
            pl.BlockSpec((None, 8, seq), lambda b, h, i: (b, SMALL_FL // 8, 0)),
        ],
        out_specs=pl.BlockSpec((blk, D_HEAD_DIM), lambda b, h, i: (b * nq + i, h)),
        compiler_params=_params("parallel", "parallel", "arbitrary"),
        name="fox",
    )(p_cd, p_cd, p_cd, cum, cum_t)


def _merge_body(ya_ref, yb_ref, yc_ref, yd_ref, ga_ref, gb_ref, gc_ref, gd_ref, wb_ref, o_ref):
    merged = ga_ref[...].astype(F32) * _dot(ya_ref[...], wb_ref[0])
    merged = merged + gb_ref[...].astype(F32) * _dot(yb_ref[...], wb_ref[1])
    merged = merged + gc_ref[...].astype(F32) * _dot(yc_ref[...], wb_ref[2])
    merged = merged + gd_ref[...].astype(F32) * _dot(yd_ref[...], wb_ref[3])
    o_ref[...] = merged.astype(o_ref.dtype)


def merge(ys, gates, w_branch, layer, tm=1024, tn=512):
    m = ys[0].shape[0]
    d = w_branch.shape[-1]
    nj = d // tn
    y_spec = pl.BlockSpec((tm, WIDTH), lambda i, j: (i, 0))
    gate_specs = [pl.BlockSpec((tm, tn), functools.partial(lambda i, j, n: (i, n * nj + j), n=n)) for n in range(4)]
    return pl.pallas_call(
        _merge_body,
        out_shape=jax.ShapeDtypeStruct((m, d), BF16),
        grid=(m // tm, nj),
        in_specs=[y_spec] * 4 + gate_specs + [pl.BlockSpec((None, 4, WIDTH, tn), lambda i, j: (layer, 0, 0, j))],
        out_specs=pl.BlockSpec((tm, tn), lambda i, j: (i, j)),
        compiler_params=_params("parallel", "arbitrary"),
        name="merge",
    )(*ys, gates, gates, gates, gates, w_branch)


def _out_proj_body(a_ref, x_hbm, gn_ref, w_ref, xo_ref, ho_ref, *, nk, tm):
    kk = pl.program_id(1)

    @pl.when(kk == 0)
    def _():
        _load_residual_rows(x_hbm, xo_ref, tm)

    xo_ref[...] += _dot(a_ref[...], w_ref[...])

    @pl.when(kk == nk - 1)
    def _():
        ho_ref[...] = _rmsnorm_rows(xo_ref[...], gn_ref[...]).astype(ho_ref.dtype)


def out_proj(a, x, w, layer, g_next, tm=1024, tk=512):
    m, d = x.shape
    nk = a.shape[1] // tk
    row_spec = pl.BlockSpec((tm, d), lambda i, kk: (i, 0))
    return pl.pallas_call(
        functools.partial(_out_proj_body, nk=nk, tm=tm),
        out_shape=[jax.ShapeDtypeStruct((m, d), F32), jax.ShapeDtypeStruct((m, d), BF16)],
        grid=(m // tm, nk),
        in_specs=[
            pl.BlockSpec((tm, tk), lambda i, kk: (i, kk)),
            pl.BlockSpec(memory_space=pl.ANY),
            pl.BlockSpec((1, d), lambda i, kk: (0, 0)),
            pl.BlockSpec((None, tk, d), lambda i, kk: (layer, kk, 0)),
        ],
        out_specs=[row_spec, row_spec],
        compiler_params=_params("parallel", "arbitrary"),
        name="out_proj",
    )(a, x, g_next.reshape(1, d), w)


def _pad_lanes(v, offset):
    return jnp.zeros((1, LANES), F32).at[0, offset:offset + v.shape[0]].set(v.astype(F32))


def kernel(x, ffn1_norm, ffn1_w_in, ffn1_w_out, mix_norm, w_mix_in, sgu_norm, sgu_w, sgu_b, conv_w, conv_b,
           dt_bias, a_log, d_skip, ssm_norm, forget_bias, w_branch, w_mix_out, ffn2_norm, ffn2_w_in, ffn2_w_out,
           final_norm):
    bsz, seq, d = x.shape
    depth = ffn1_norm.shape[0]
    m = bsz * seq
    x = x.reshape(m, d)
    tables = _retention_tables(seq)
    h = rmsnorm(x, ffn1_norm[0], BF16)
    for l in range(depth):
        x, h = ffn(h, x, ffn1_w_in, ffn1_w_out, l, mix_norm[l], emit_x=True, h_dtype=BF16)

        w_small = jnp.concatenate(
            [w_mix_in[l, :, COL_FL:COL_G], w_mix_in[l, :, COL_DT:COL_C],
             jnp.zeros((d, LANES - D_HEADS - B_HEADS), F32)], axis=1)
        p_ab = matmul(h, w_mix_in, N_AB, layer=l, out_dtype=BF16)
        p_small = matmul(h, w_small, LANES)
        p_cd = matmul(h, realign(w_mix_in, l, COL_C, N_CD), N_CD, out_dtype=BF16)
        gates = matmul(h, realign(w_mix_in, l, COL_G, 4 * d), 4 * d, act="sigmoid", out_dtype=BF16)

        b_full = jnp.repeat(sgu_b[l].T, LANES, axis=1)
        y_a = sgu(p_ab, sgu_norm[l], sgu_w, b_full, l)
        y_b = ssd(p_ab, p_small, conv_w, conv_b, _pad_lanes(dt_bias[l], SMALL_DT), _pad_lanes(a_log[l], SMALL_DT),
                  jnp.repeat(d_skip[l], B_HEAD_DIM).reshape(1, WIDTH), ssm_norm[l], l, bsz, seq)
        y_c = retention(p_cd, tables, bsz, seq)
        cum, cum_t = forget_cumsum(p_small, _pad_lanes(forget_bias[l], SMALL_FL), bsz, seq)
        y_d = fox(p_cd, cum, cum_t, bsz, seq)

        merged = merge((y_a, y_b, y_c, y_d), gates, w_branch, l)
        x, h = out_proj(merged, x, w_mix_out, l, ffn2_norm[l])
        if l + 1 < depth:
            x, h = ffn(h, x, ffn2_w_in, ffn2_w_out, l, ffn1_norm[l + 1], emit_x=True, h_dtype=BF16)
        else:
            h = ffn(h, x, ffn2_w_in, ffn2_w_out, l, final_norm, emit_x=False, h_dtype=F32)
    return h.reshape(bsz, seq, d)
```

```python
import functools

import jax
import jax.numpy as jnp
from jax import lax
from jax.experimental import pallas as pl
from jax.experimental.pallas import tpu as pltpu

F32 = jnp.float32
BF16 = jnp.bfloat16

EPS = 1e-6
CHUNK = 128
LANES = 128
VMEM_LIMIT = 56 * 1024 * 1024

D_MODEL = 2048
D_FF = 5632
WIDTH = 1024
A_GROUPS = 8
B_HEADS = 16
B_HEAD_DIM = 64
B_STATE = 128
B_GROUPS = 2
B_CONV = 4
B_CONV_DIM = WIDTH + 2 * B_GROUPS * B_STATE
C_HEADS = 8
C_HEAD_QK = 64
C_QK = C_HEADS * C_HEAD_QK
D_HEADS = 8
D_HEAD_DIM = 128
ROPE_BASE = 10000.0

COL_DT = 2 * WIDTH + WIDTH + B_CONV_DIM
COL_C = COL_DT + B_HEADS
COL_FL = COL_C + 2 * C_QK + 2 * WIDTH + 3 * WIDTH
COL_G = COL_FL + D_HEADS
N_AB = COL_DT
N_CD = COL_FL - COL_C
SMALL_FL = 0
SMALL_DT = D_HEADS


def _params(*sem):
    return pltpu.CompilerParams(dimension_semantics=sem, vmem_limit_bytes=VMEM_LIMIT)


def _dot(a, b):
    return jnp.dot(a.astype(BF16), b.astype(BF16), preferred_element_type=F32)


def _dot_nt(a, b):
    return lax.dot_general(a.astype(BF16), b.astype(BF16), (((1,), (1,)), ((), ())),
                           preferred_element_type=F32)


def _split3(x):
    hi = x.astype(BF16)
    rest = x - hi.astype(F32)
    mid = rest.astype(BF16)
    lo = (rest - mid.astype(F32)).astype(BF16)
    return lo, mid, hi


def _dot_f32_by_mask(x, mask01):
    mask01 = mask01.astype(BF16)
    lo, mid, hi = (jnp.dot(t, mask01, preferred_element_type=F32) for t in _split3(x))
    return lo + mid + hi


def _mask_dot_f32(mask01, x):
    mask01 = mask01.astype(BF16)
    lo, mid, hi = (jnp.dot(mask01, t, preferred_element_type=F32) for t in _split3(x))
    return lo + mid + hi


def _sigmoid(x):
    return 0.5 * jnp.tanh(0.5 * x) + 0.5


def _silu(x):
    return x * _sigmoid(x)


def _softplus(x):
    return jnp.maximum(x, 0.0) + jnp.log1p(jnp.exp(-jnp.abs(x)))


def _lower_tri(n):
    return lax.broadcasted_iota(jnp.int32, (n, n), 1) <= lax.broadcasted_iota(jnp.int32, (n, n), 0)


def _rmsnorm_rows(x, g):
    return x * lax.rsqrt(jnp.mean(x * x, axis=-1, keepdims=True) + EPS) * g


def _rmsnorm_body(x_ref, g_ref, o_ref):
    o_ref[...] = _rmsnorm_rows(x_ref[...], g_ref[...]).astype(o_ref.dtype)


def rmsnorm(x, g, out_dtype, tm=512):
    m, d = x.shape
    return pl.pallas_call(
        _rmsnorm_body,
        out_shape=jax.ShapeDtypeStruct((m, d), out_dtype),
        grid=(m // tm,),
        in_specs=[pl.BlockSpec((tm, d), lambda i: (i, 0)), pl.BlockSpec((1, d), lambda i: (0, 0))],
        out_specs=pl.BlockSpec((tm, d), lambda i: (i, 0)),
        compiler_params=_params("parallel"),
        name="rmsnorm",
    )(x, g.reshape(1, d))


def _load_residual_rows(x_hbm, acc_ref, tm):
    rows = pl.ds(pl.multiple_of(pl.program_id(0) * tm, tm), tm)
    pltpu.sync_copy(x_hbm.at[rows, :], acc_ref)


def _ffn_body(h_ref, x_hbm, gn_ref, wg_ref, wu_ref, wo_ref, *refs, nf, tm, emit_x):
    xo_ref, ho_ref = refs if emit_x else (refs[1], refs[0])
    f = pl.program_id(1)

    @pl.when(f == 0)
    def _():
        _load_residual_rows(x_hbm, xo_ref, tm)

    h = h_ref[...]
    gate = _dot(h, wg_ref[...])
    up = _dot(h, wu_ref[...])
    xo_ref[...] += _dot(0.5 * _silu(gate) * up, wo_ref[...])

    @pl.when(f == nf - 1)
    def _():
        ho_ref[...] = _rmsnorm_rows(xo_ref[...], gn_ref[...]).astype(ho_ref.dtype)


def ffn(h, x, w_in, w_out, layer, g_next, *, emit_x, h_dtype, tm=1024, tf=512):
    m, d = x.shape
    nf = D_FF // tf
    row_spec = pl.BlockSpec((tm, d), lambda i, f: (i, 0), pipeline_mode=pl.Buffered(1))
    out_shape = [jax.ShapeDtypeStruct((m, d), h_dtype)]
    out_specs = [row_spec]
    scratch = [pltpu.VMEM((tm, d), F32)]
    if emit_x:
        out_shape.insert(0, jax.ShapeDtypeStruct((m, d), F32))
        out_specs.insert(0, row_spec)
        scratch = []
    res = pl.pallas_call(
        functools.partial(_ffn_body, nf=nf, tm=tm, emit_x=emit_x),
        out_shape=out_shape,
        grid=(m // tm, nf),
        in_specs=[
            row_spec,
            pl.BlockSpec(memory_space=pl.ANY),
            pl.BlockSpec((1, d), lambda i, f: (0, 0)),
            pl.BlockSpec((None, d, tf), lambda i, f: (layer, 0, f)),
            pl.BlockSpec((None, d, tf), lambda i, f: (layer, 0, nf + f)),
            pl.BlockSpec((None, tf, d), lambda i, f: (layer, f, 0)),
        ],
        out_specs=out_specs,
        scratch_shapes=scratch,
        compiler_params=_params("parallel", "arbitrary"),
        name="ffn",
    )(h, x, g_next.reshape(1, d), w_in, w_in, w_out)
    return res if emit_x else res[0]


def _proj_body(h_ref, w_ref, o_ref, *, act):
    acc = _dot_nt(h_ref[...], w_ref[0])
    if act == "sigmoid":
        acc = _sigmoid(acc)
    o_ref[...] = acc.astype(o_ref.dtype)


def proj(h, w_t, layer, first_row, n, *, act=None, tm=2048, tn=512):
    m, k = h.shape
    assert first_row % 8 == 0 and n % tn == 0
    w_spec = pl.BlockSpec((pl.Element(1), pl.Element(tn), pl.Element(k)),
                          lambda i, j: (layer, pl.multiple_of(first_row + j * tn, 8), 0))
    return pl.pallas_call(
        functools.partial(_proj_body, act=act),
        out_shape=jax.ShapeDtypeStruct((m, n), BF16),
        grid=(m // tm, n // tn),
        in_specs=[pl.BlockSpec((tm, k), lambda i, j: (i, 0)), w_spec],
        out_specs=pl.BlockSpec((tm, tn), lambda i, j: (i, j)),
        compiler_params=_params("parallel", "arbitrary"),
        name="proj",
    )(h, w_t)


def _proj_small_body(h_ref, wf_ref, wd_ref, o_ref):
    k = h_ref.shape[1]
    pad = jnp.zeros((LANES - D_HEADS - B_HEADS, k), F32)
    w = jnp.concatenate([wf_ref[...], wd_ref[...], pad], axis=0)
    o_ref[...] = _dot_nt(h_ref[...], w)


def proj_small(h, w_t, layer, tm=2048):
    m, k = h.shape
    return pl.pallas_call(
        _proj_small_body,
        out_shape=jax.ShapeDtypeStruct((m, LANES), F32),
        grid=(m // tm,),
        in_specs=[
            pl.BlockSpec((tm, k), lambda i: (i, 0)),
            pl.BlockSpec((None, D_HEADS, k), lambda i: (layer, COL_FL // D_HEADS, 0)),
            pl.BlockSpec((None, B_HEADS, k), lambda i: (layer, COL_DT // B_HEADS, 0)),
        ],
        out_specs=pl.BlockSpec((tm, LANES), lambda i: (i, 0)),
        compiler_params=_params("parallel"),
        name="proj_small",
    )(h, w_t, w_t)


def _sgu_body(p_ref, lng_ref, w_ref, b_ref, o_ref):
    causal = _lower_tri(CHUNK)
    for ci in range(p_ref.shape[0] // CHUNK):
        rows = slice(ci * CHUNK, (ci + 1) * CHUNK)
        p = jax.nn.gelu(p_ref[rows, :].astype(F32))
        u = p[:, :WIDTH]
        v = p[:, WIDTH:]
        mu = jnp.mean(v, axis=-1, keepdims=True)
        var = jnp.mean(jnp.square(v - mu), axis=-1, keepdims=True)
        vn = (v - mu) * lax.rsqrt(var + EPS) * lng_ref[...]
        for g in range(A_GROUPS):
            cols = slice(g * LANES, (g + 1) * LANES)
            w = jnp.where(causal, w_ref[g], 0.0)
            mixed = _dot(w, vn[:, cols]) + b_ref[:, cols]
            o_ref[rows, cols] = (u[:, cols] * mixed).astype(o_ref.dtype)


def sgu(p_ab, ln_g, w_s, b_full, layer, rows=4 * CHUNK):
    m = p_ab.shape[0]
    return pl.pallas_call(
        _sgu_body,
        out_shape=jax.ShapeDtypeStruct((m, WIDTH), BF16),
        grid=(m // rows,),
        in_specs=[
            pl.BlockSpec((rows, 2 * WIDTH), lambda c: (c, 0)),
            pl.BlockSpec((1, WIDTH), lambda c: (0, 0)),
            pl.BlockSpec((None, A_GROUPS, CHUNK, CHUNK), lambda c: (layer, 0, 0, 0)),
            pl.BlockSpec((CHUNK, WIDTH), lambda c: (0, 0)),
        ],
        out_specs=pl.BlockSpec((rows, WIDTH), lambda c: (c, 0)),
        compiler_params=_params("parallel"),
        name="sgu",
    )(p_ab, ln_g.reshape(1, WIDTH), w_s, b_full)


def _ssd_body(z_ref, xbc_ref, small_ref, cw_ref, cb_ref, dtb_ref, alog_ref, dsk_ref, ng_ref, o_ref,
              xpad_ref, state_ref, y_ref):
    c = pl.program_id(1)

    @pl.when(c == 0)
    def _():
        xpad_ref[0:8, :] = jnp.zeros((8, B_CONV_DIM), F32)
        state_ref[...] = jnp.zeros_like(state_ref)

    xpad_ref[8:8 + CHUNK, :] = xbc_ref[...].astype(F32)
    conv = cb_ref[...]
    for j in range(B_CONV):
        conv = conv + cw_ref[j:j + 1, :] * xpad_ref[pl.ds(8 - (B_CONV - 1) + j, CHUNK), :]
    xpad_ref[0:8, :] = xpad_ref[CHUNK:CHUNK + 8, :]
    xbc = _silu(conv)
    xs = xbc[:, :WIDTH]
    bm = xbc[:, WIDTH:WIDTH + B_GROUPS * B_STATE]
    cm = xbc[:, WIDTH + B_GROUPS * B_STATE:]

    dt = _softplus(small_ref[...] + dtb_ref[...])
    a = -jnp.exp(alog_ref[...])
    causal = _lower_tri(CHUNK)
    acum = _mask_dot_f32(causal, dt * a)
    acum_t = acum.T
    expand = (lax.broadcasted_iota(jnp.int32, (LANES, WIDTH), 0) - SMALL_DT
              == lax.broadcasted_iota(jnp.int32, (LANES, WIDTH), 1) // B_HEAD_DIM)
    dt_e = _dot_f32_by_mask(dt, expand)
    acum_e = _dot_f32_by_mask(acum, expand)
    eacum_e = jnp.exp(acum_e)
    dend_e = jnp.exp(acum_e[CHUNK - 1:CHUNK, :] - acum_e)
    xdt = xs * dt_e
    xend = xdt * dend_e
    lane = lax.broadcasted_iota(jnp.int32, (CHUNK, LANES), 1)
    low_half = lane < B_HEAD_DIM
    hpg = B_HEADS // B_GROUPS
    gw = WIDTH // B_GROUPS

    for g in range(B_GROUPS):
        bg = bm[:, g * B_STATE:(g + 1) * B_STATE]
        cg = cm[:, g * B_STATE:(g + 1) * B_STATE]
        cb = _dot_nt(cg, bg)
        gcols = slice(g * gw, (g + 1) * gw)
        prev = state_ref[:, gcols]
        y_off = _dot(cg, prev) * eacum_e[:, gcols]
        for pair in range(hpg // 2):
            pcols = slice(g * gw + pair * LANES, g * gw + (pair + 1) * LANES)
            x_pair = xdt[:, pcols]
            y_pair = jnp.zeros((CHUNK, LANES), F32)
            for half in range(2):
                hl = SMALL_DT + g * hpg + 2 * pair + half
                seg = acum[:, hl:hl + 1] - acum_t[hl:hl + 1, :]
                decay = jnp.exp(jnp.where(causal, seg, -jnp.inf))
                x_half = jnp.where(low_half if half == 0 else ~low_half, x_pair, 0.0)
                y_pair = y_pair + _dot(cb * decay, x_half)
            y_pair = y_pair + y_off[:, pair * LANES:(pair + 1) * LANES] + xs[:, pcols] * dsk_ref[:, pcols]
            y_ref[:, pcols] = y_pair
        st = _dot(bg.T, xend[:, gcols])
        state_ref[:, gcols] = prev * eacum_e[CHUNK - 1:CHUNK, gcols] + st

    y = y_ref[...] * _silu(z_ref[...].astype(F32))
    for g in range(B_GROUPS):
        gcols = slice(g * gw, (g + 1) * gw)
        yg = y[:, gcols]
        yg = yg * lax.rsqrt(jnp.mean(yg * yg, axis=-1, keepdims=True) + EPS)
        o_ref[:, gcols] = (yg * ng_ref[:, gcols]).astype(o_ref.dtype)


def ssd(p_ab, p_small, conv_w, conv_b, dtb_row, alog_row, dskip_row, norm_g, layer, bsz, seq):
    nc = seq // CHUNK
    m = bsz * seq
    row = lambda b, c: b * nc + c
    return pl.pallas_call(
        _ssd_body,
        out_shape=jax.ShapeDtypeStruct((m, WIDTH), BF16),
        grid=(bsz, nc),
        in_specs=[
            pl.BlockSpec((CHUNK, WIDTH), lambda b, c: (row(b, c), 2)),
            pl.BlockSpec((CHUNK, B_CONV_DIM), lambda b, c: (row(b, c), 2)),
            pl.BlockSpec((CHUNK, LANES), lambda b, c: (row(b, c), 0)),
            pl.BlockSpec((None, B_CONV, B_CONV_DIM), lambda b, c: (layer, 0, 0)),
            pl.BlockSpec((None, 1, B_CONV_DIM), lambda b, c: (layer, 0, 0)),
            pl.BlockSpec((1, LANES), lambda b, c: (0, 0)),
            pl.BlockSpec((1, LANES), lambda b, c: (0, 0)),
            pl.BlockSpec((1, WIDTH), lambda b, c: (0, 0)),
            pl.BlockSpec((1, WIDTH), lambda b, c: (0, 0)),
        ],
        out_specs=pl.BlockSpec((CHUNK, WIDTH), lambda b, c: (row(b, c), 0)),
        scratch_shapes=[pltpu.VMEM((CHUNK + 8, B_CONV_DIM), F32), pltpu.VMEM((B_STATE, WIDTH), F32),
                        pltpu.VMEM((CHUNK, WIDTH), F32)],
        compiler_params=_params("parallel", "arbitrary"),
        name="ssd",
    )(p_ab, p_ab, p_small, conv_w, conv_b.reshape(conv_b.shape[0], 1, B_CONV_DIM), dtb_row, alog_row,
      dskip_row, norm_g.reshape(1, WIDTH))


def _retention_tables(seq):
    half = C_HEAD_QK // 2
    inv_freq = 1.0 / (ROPE_BASE ** (jnp.arange(half, dtype=F32) / half))
    ang = jnp.arange(seq, dtype=F32)[:, None] * inv_freq[None, :]
    cos = jnp.tile(jnp.cos(ang), (1, 2 * LANES // C_HEAD_QK))
    sin = jnp.tile(jnp.concatenate([-jnp.sin(ang), jnp.sin(ang)], axis=1), (1, LANES // C_HEAD_QK))
    log_gamma = jnp.log(1.0 - 2.0 ** (-5.0 - jnp.arange(C_HEADS, dtype=F32)))
    idx = jnp.arange(CHUNK, dtype=F32)
    mask = idx[:, None] >= idx[None, :]
    rel = jnp.where(mask, idx[:, None] - idx[None, :], 0.0)
    intra = (jnp.exp(rel[..., None] * log_gamma) * mask[..., None]).transpose(2, 0, 1)
    k_decay = jnp.repeat(jnp.exp((CHUNK - 1.0 - idx)[:, None] * log_gamma), C_HEAD_QK, axis=1)
    q_decay = jnp.repeat(jnp.exp((idx + 1.0)[:, None] * log_gamma), C_HEAD_QK, axis=1)
    chunk_decay = jnp.broadcast_to(jnp.repeat(jnp.exp(CHUNK * log_gamma), C_HEAD_QK)[:, None],
                                   (C_QK, LANES))
    return cos, sin, intra, k_decay, q_decay, chunk_decay


def _retention_body(qk_ref, v_ref, g_ref, cos_ref, sin_ref, intra_ref, kdec_ref, qdec_ref, cdec_ref,
                    small_ref, fb_ref, o_ref, cum_ref, cum_t_ref, state_ref, carry_ref):
    c = pl.program_id(1)

    @pl.when(c == 0)
    def _():
        state_ref[...] = jnp.zeros_like(state_ref)
        carry_ref[...] = jnp.zeros_like(carry_ref)

    log_f = -_softplus(-(small_ref[...] + fb_ref[...]))
    cum = _mask_dot_f32(_lower_tri(CHUNK), log_f) + carry_ref[...]
    carry_ref[...] = cum[CHUNK - 1:CHUNK, :]
    cum_ref[...] = cum
    cum_t_ref[...] = cum.T

    lane = lax.broadcasted_iota(jnp.int32, (CHUNK, LANES), 1)
    first_half = (lane % C_HEAD_QK) < (C_HEAD_QK // 2)
    low_head = lane < C_HEAD_QK
    low_rows = lax.broadcasted_iota(jnp.int32, (LANES, LANES), 0) < C_HEAD_QK
    cos = cos_ref[...]
    sin = sin_ref[...]

    def rotary(x):
        partner = jnp.where(first_half, pltpu.roll(x, LANES - C_HEAD_QK // 2, 1), pltpu.roll(x, C_HEAD_QK // 2, 1))
        return x * cos + partner * sin

    for pair in range(C_HEADS // 2):
        pc = slice(pair * LANES, (pair + 1) * LANES)
        q2 = rotary(qk_ref[:, pc].astype(F32))
        k2 = rotary(qk_ref[:, C_QK + pair * LANES:C_QK + (pair + 1) * LANES].astype(F32)) * (C_HEAD_QK ** -0.5)
        qd2 = q2 * qdec_ref[:, pc]
        kd2_t = (k2 * kdec_ref[:, pc]).T
        prev = state_ref[pc, :]
        new_state = []
        for half in range(2):
            h = 2 * pair + half
            hc = slice(h * LANES, (h + 1) * LANES)
            sel = low_head if half == 0 else ~low_head
            scores = _dot_nt(jnp.where(sel, q2, 0.0), k2) * intra_ref[h]
            vh = v_ref[:, hc]
            y = _dot(scores, vh) + _dot(jnp.where(sel, qd2, 0.0), prev)
            mu = jnp.mean(y, axis=-1, keepdims=True)
            var = jnp.mean(jnp.square(y - mu), axis=-1, keepdims=True)
            y = (y - mu) * lax.rsqrt(var + EPS)
            o_ref[:, hc] = (_silu(g_ref[:, hc].astype(F32)) * y).astype(o_ref.dtype)
            new_state.append(_dot(kd2_t, vh))
        state_ref[pc, :] = prev * cdec_ref[pc, :] + jnp.where(low_rows, new_state[0], new_state[1])


def retention(p_cd, p_small, fb_row, tables, bsz, seq):
    nc = seq // CHUNK
    m = bsz * seq
    cos, sin, intra, k_decay, q_decay, chunk_decay = tables
    row = lambda b, c: b * nc + c
    const2 = lambda b, c: (0, 0)
    return pl.pallas_call(
        _retention_body,
        out_shape=[jax.ShapeDtypeStruct((m, WIDTH), BF16), jax.ShapeDtypeStruct((m, LANES), F32),
                   jax.ShapeDtypeStruct((bsz, LANES, seq), F32)],
        grid=(bsz, nc),
        in_specs=[
            pl.BlockSpec((CHUNK, 2 * C_QK), lambda b, c: (row(b, c), 0)),
            pl.BlockSpec((CHUNK, WIDTH), lambda b, c: (row(b, c), 1)),
            pl.BlockSpec((CHUNK, WIDTH), lambda b, c: (row(b, c), 2)),
            pl.BlockSpec((CHUNK, LANES), lambda b, c: (c, 0)),
            pl.BlockSpec((CHUNK, LANES), lambda b, c: (c, 0)),
            pl.BlockSpec((C_HEADS, CHUNK, CHUNK), lambda b, c: (0, 0, 0)),
            pl.BlockSpec((CHUNK, C_QK), const2),
            pl.BlockSpec((CHUNK, C_QK), const2),
            pl.BlockSpec((C_QK, LANES), const2),
            pl.BlockSpec((CHUNK, LANES), lambda b, c: (row(b, c), 0)),
            pl.BlockSpec((1, LANES), const2),
        ],
        out_specs=[pl.BlockSpec((CHUNK, WIDTH), lambda b, c: (row(b, c), 0)),
                   pl.BlockSpec((CHUNK, LANES), lambda b, c: (row(b, c), 0)),
                   pl.BlockSpec((None, LANES, CHUNK), lambda b, c: (b, 0, c))],
        scratch_shapes=[pltpu.VMEM((C_QK, LANES), F32), pltpu.VMEM((1, LANES), F32)],
        compiler_params=_params("parallel", "arbitrary"),
        name="retention",
    )(p_cd, p_cd, p_cd, cos, sin, intra, k_decay, q_decay, chunk_decay, p_small, fb_row)


def _fox_body(q_ref, k_ref, v_ref, cum_ref, cum_t_ref, o_ref, *, blk):
    h = pl.program_id(1)
    qi = pl.program_id(2)
    q = q_ref[...].astype(BF16)
    lane = lax.broadcasted_iota(jnp.int32, (blk, LANES), 1)
    cq = jnp.sum(jnp.where(lane == SMALL_FL + h, cum_ref[...], 0.0), axis=-1, keepdims=True)
    scale = D_HEAD_DIM ** -0.5

    def scores(kj):
        ks = pl.multiple_of(kj * blk, blk)
        s = _dot_nt(q, k_ref[pl.ds(ks, blk), :]) * scale
        return s + cq - cum_t_ref[pl.ds(SMALL_FL + h, 1), pl.ds(ks, blk)], ks

    def update(carry, s, ks):
        m_prev, l_prev, acc = carry
        m_new = jnp.maximum(m_prev, jnp.max(s, axis=-1, keepdims=True))
        alpha = jnp.exp(m_prev - m_new)
        p = jnp.exp(s - m_new)
        l_new = alpha * l_prev + jnp.sum(p, axis=-1, keepdims=True)
        acc = alpha * acc + _dot(p, v_ref[pl.ds(ks, blk), :])
        return m_new, l_new, acc

    def full_block(kj, carry):
        s, ks = scores(kj)
        return update(carry, s, ks)

    init = (jnp.full((blk, 1), -jnp.inf, F32), jnp.zeros((blk, 1), F32), jnp.zeros((blk, D_HEAD_DIM), F32))
    carry = lax.fori_loop(0, qi, full_block, init)
    s, ks = scores(qi)
    s = jnp.where(_lower_tri(blk), s, -jnp.inf)
    _, l_fin, acc = update(carry, s, ks)
    o_ref[...] = (acc / l_fin).astype(o_ref.dtype)


def fox(p_cd, cum, cum_t, bsz, seq, blk=512):
    nq = seq // blk
    col0 = (2 * C_QK + 2 * WIDTH) // D_HEAD_DIM
    nh = D_HEADS
    return pl.pallas_call(
        functools.partial(_fox_body, blk=blk),
        out_shape=jax.ShapeDtypeStruct((bsz * seq, WIDTH), BF16),
        grid=(bsz, nh, nq),
        in_specs=[
            pl.BlockSpec((blk, D_HEAD_DIM), lambda b, h, i: (b * nq + i, col0 + h)),
            pl.BlockSpec((seq, D_HEAD_DIM), lambda b, h, i: (b, col0 + nh + h)),
            pl.BlockSpec((seq, D_HEAD_DIM), lambda b, h, i: (b, col0 + 2 * nh + h)),
            pl.BlockSpec((blk, LANES), lambda b, h, i: (b * nq + i, 0)),
            pl.BlockSpec((None, 8, seq), lambda b, h, i: (b, SMALL_FL // 8, 0)),
        ],
        out_specs=pl.BlockSpec((blk, D_HEAD_DIM), lambda b, h, i: (b * nq + i, h)),
        compiler_params=_params("parallel", "parallel", "arbitrary"),
        name="fox",
    )(p_cd, p_cd, p_cd, cum, cum_t)


def _merge_body(ya_ref, yb_ref, yc_ref, yd_ref, ga_ref, gb_ref, gc_ref, gd_ref, wb_ref, o_ref):
    merged = ga_ref[...].astype(F32) * _dot(ya_ref[...], wb_ref[0])
    merged = merged + gb_ref[...].astype(F32) * _dot(yb_ref[...], wb_ref[1])
    merged = merged + gc_ref[...].astype(F32) * _dot(yc_ref[...], wb_ref[2])
    merged = merged + gd_ref[...].astype(F32) * _dot(yd_ref[...], wb_ref[3])
    o_ref[...] = merged.astype(o_ref.dtype)


def merge(ys, gates, w_branch, layer, tm=1024, tn=512):
    m = ys[0].shape[0]
    d = w_branch.shape[-1]
    nj = d // tn
    y_spec = pl.BlockSpec((tm, WIDTH), lambda i, j: (i, 0))
    gate_specs = [pl.BlockSpec((tm, tn), functools.partial(lambda i, j, n: (i, n * nj + j), n=n)) for n in range(4)]
    return pl.pallas_call(
        _merge_body,
        out_shape=jax.ShapeDtypeStruct((m, d), BF16),
        grid=(m // tm, nj),
        in_specs=[y_spec] * 4 + gate_specs + [pl.BlockSpec((None, 4, WIDTH, tn), lambda i, j: (layer, 0, 0, j))],
        out_specs=pl.BlockSpec((tm, tn), lambda i, j: (i, j)),
        compiler_params=_params("parallel", "arbitrary"),
        name="merge",
    )(*ys, gates, gates, gates, gates, w_branch)


def _out_proj_body(a_ref, x_ref, gn_ref, w_ref, xo_ref, ho_ref):
    x_new = x_ref[...] + _dot(a_ref[...], w_ref[...])
    xo_ref[...] = x_new
    ho_ref[...] = _rmsnorm_rows(x_new, gn_ref[...]).astype(ho_ref.dtype)


def out_proj(a, x, w, layer, g_next, tm=512):
    m, d = x.shape
    k = a.shape[1]
    row_spec = pl.BlockSpec((tm, d), lambda i: (i, 0))
    return pl.pallas_call(
        _out_proj_body,
        out_shape=[jax.ShapeDtypeStruct((m, d), F32), jax.ShapeDtypeStruct((m, d), BF16)],
        grid=(m // tm,),
        in_specs=[
            pl.BlockSpec((tm, k), lambda i: (i, 0)),
            row_spec,
            pl.BlockSpec((1, d), lambda i: (0, 0)),
            pl.BlockSpec((None, k, d), lambda i: (layer, 0, 0), pipeline_mode=pl.Buffered(1)),
        ],
        out_specs=[row_spec, row_spec],
        compiler_params=_params("parallel"),
        name="out_proj",
    )(a, x, g_next.reshape(1, d), w)


def _pad_lanes(v, offset):
    return jnp.zeros((1, LANES), F32).at[0, offset:offset + v.shape[0]].set(v.astype(F32))


def kernel(x, ffn1_norm, ffn1_w_in, ffn1_w_out, mix_norm, w_mix_in, sgu_norm, sgu_w, sgu_b, conv_w, conv_b,
           dt_bias, a_log, d_skip, ssm_norm, forget_bias, w_branch, w_mix_out, ffn2_norm, ffn2_w_in, ffn2_w_out,
           final_norm):
    bsz, seq, d = x.shape
    depth = ffn1_norm.shape[0]
    m = bsz * seq
    x = x.reshape(m, d)
    tables = _retention_tables(seq)
    w_t = jnp.swapaxes(w_mix_in, 1, 2)
    h = rmsnorm(x, ffn1_norm[0], BF16)
    for l in range(depth):
        x, h = ffn(h, x, ffn1_w_in, ffn1_w_out, l, mix_norm[l], emit_x=True, h_dtype=BF16)

        p_ab = proj(h, w_t, l, 0, N_AB)
        p_small = proj_small(h, w_t, l)
        p_cd = proj(h, w_t, l, COL_C, N_CD)
        gates = proj(h, w_t, l, COL_G, 4 * d, act="sigmoid")

        b_full = jnp.repeat(sgu_b[l].T, LANES, axis=1)
        y_a = sgu(p_ab, sgu_norm[l], sgu_w, b_full, l)
        y_b = ssd(p_ab, p_small, conv_w, conv_b, _pad_lanes(dt_bias[l], SMALL_DT), _pad_lanes(a_log[l], SMALL_DT),
                  jnp.repeat(d_skip[l], B_HEAD_DIM).reshape(1, WIDTH), ssm_norm[l], l, bsz, seq)
        y_c, cum, cum_t = retention(p_cd, p_small, _pad_lanes(forget_bias[l], SMALL_FL), tables, bsz, seq)
        y_d = fox(p_cd, cum, cum_t, bsz, seq)

        merged = merge((y_a, y_b, y_c, y_d), gates, w_branch, l)
        x, h = out_proj(merged, x, w_mix_out, l, ffn2_norm[l])
        if l + 1 < depth:
            x, h = ffn(h, x, ffn2_w_in, ffn2_w_out, l, ffn1_norm[l + 1], emit_x=True, h_dtype=BF16)
        else:
            h = ffn(h, x, ffn2_w_in, ffn2_w_out, l, final_norm, emit_x=False, h_dtype=F32)
    return h.reshape(bsz, seq, d)
```

```python
import functools

import jax
import jax.numpy as jnp
from jax import lax
from jax.experimental import pallas as pl
from jax.experimental.pallas import tpu as pltpu

F32 = jnp.float32
BF16 = jnp.bfloat16

EPS = 1e-6
LOG2E = 1.4426950408889634
CHUNK = 128
LANES = 128
VMEM_LIMIT = 56 * 1024 * 1024

D_MODEL = 2048
D_FF = 5632
WIDTH = 1024
A_GROUPS = 8
B_HEADS = 16
B_HEAD_DIM = 64
B_STATE = 128
B_GROUPS = 2
B_CONV = 4
B_CONV_DIM = WIDTH + 2 * B_GROUPS * B_STATE
C_HEADS = 8
C_HEAD_QK = 64
C_QK = C_HEADS * C_HEAD_QK
D_HEADS = 8
D_HEAD_DIM = 128
ROPE_BASE = 10000.0

COL_DT = 2 * WIDTH + WIDTH + B_CONV_DIM
COL_C = COL_DT + B_HEADS
COL_FL = COL_C + 2 * C_QK + 2 * WIDTH + 3 * WIDTH
COL_G = COL_FL + D_HEADS
N_AB = COL_DT
N_CD = COL_FL - COL_C
SMALL_FL = 0
SMALL_DT = D_HEADS


def _params(*sem):
    return pltpu.CompilerParams(dimension_semantics=sem, vmem_limit_bytes=VMEM_LIMIT)


def _dot(a, b):
    return jnp.dot(a.astype(BF16), b.astype(BF16), preferred_element_type=F32)


def _dot_nt(a, b):
    return lax.dot_general(a.astype(BF16), b.astype(BF16), (((1,), (1,)), ((), ())),
                           preferred_element_type=F32)


def _split3(x):
    hi = x.astype(BF16)
    rest = x - hi.astype(F32)
    mid = rest.astype(BF16)
    lo = (rest - mid.astype(F32)).astype(BF16)
    return lo, mid, hi


def _dot_f32_by_mask(x, mask01):
    mask01 = mask01.astype(BF16)
    lo, mid, hi = (jnp.dot(t, mask01, preferred_element_type=F32) for t in _split3(x))
    return lo + mid + hi


def _mask_dot_f32(mask01, x):
    mask01 = mask01.astype(BF16)
    lo, mid, hi = (jnp.dot(mask01, t, preferred_element_type=F32) for t in _split3(x))
    return lo + mid + hi


def _sigmoid(x):
    return 0.5 * jnp.tanh(0.5 * x) + 0.5


def _silu(x):
    return x * _sigmoid(x)


def _softplus(x):
    return jnp.maximum(x, 0.0) + jnp.log1p(jnp.exp(-jnp.abs(x)))


def _lower_tri(n):
    return lax.broadcasted_iota(jnp.int32, (n, n), 1) <= lax.broadcasted_iota(jnp.int32, (n, n), 0)


def _rmsnorm_rows(x, g):
    return x * lax.rsqrt(jnp.mean(x * x, axis=-1, keepdims=True) + EPS) * g


def _rmsnorm_body(x_ref, g_ref, o_ref):
    o_ref[...] = _rmsnorm_rows(x_ref[...], g_ref[...]).astype(o_ref.dtype)


def rmsnorm(x, g, out_dtype, tm=512):
    m, d = x.shape
    return pl.pallas_call(
        _rmsnorm_body,
        out_shape=jax.ShapeDtypeStruct((m, d), out_dtype),
        grid=(m // tm,),
        in_specs=[pl.BlockSpec((tm, d), lambda i: (i, 0)), pl.BlockSpec((1, d), lambda i: (0, 0))],
        out_specs=pl.BlockSpec((tm, d), lambda i: (i, 0)),
        compiler_params=_params("parallel"),
        name="rmsnorm",
    )(x, g.reshape(1, d))


def _load_residual_rows(x_hbm, acc_ref, tm):
    rows = pl.ds(pl.multiple_of(pl.program_id(0) * tm, tm), tm)
    pltpu.sync_copy(x_hbm.at[rows, :], acc_ref)


def _ffn_body(h_ref, x_hbm, gn_ref, wg_ref, wu_ref, wo_ref, *refs, nf, tm, emit_x):
    xo_ref, ho_ref = refs if emit_x else (refs[1], refs[0])
    f = pl.program_id(1)

    @pl.when(f == 0)
    def _():
        _load_residual_rows(x_hbm, xo_ref, tm)

    tf = wg_ref.shape[1]
    gu = _dot(h_ref[...], jnp.concatenate([wg_ref[...].astype(BF16), wu_ref[...].astype(BF16)], axis=1))
    xo_ref[...] += _dot(0.5 * _silu(gu[:, :tf]) * gu[:, tf:], wo_ref[...])

    @pl.when(f == nf - 1)
    def _():
        ho_ref[...] = _rmsnorm_rows(xo_ref[...], gn_ref[...]).astype(ho_ref.dtype)


def ffn(h, x, w_in, w_out, layer, g_next, *, emit_x, h_dtype, tm=2048, tf=256):
    m, d = x.shape
    nf = D_FF // tf
    row_spec = pl.BlockSpec((tm, d), lambda i, f: (i, 0), pipeline_mode=pl.Buffered(1))
    out_shape = [jax.ShapeDtypeStruct((m, d), h_dtype)]
    out_specs = [row_spec]
    scratch = [pltpu.VMEM((tm, d), F32)]
    if emit_x:
        out_shape.insert(0, jax.ShapeDtypeStruct((m, d), F32))
        out_specs.insert(0, row_spec)
        scratch = []
    res = pl.pallas_call(
        functools.partial(_ffn_body, nf=nf, tm=tm, emit_x=emit_x),
        out_shape=out_shape,
        grid=(m // tm, nf),
        in_specs=[
            row_spec,
            pl.BlockSpec(memory_space=pl.ANY),
            pl.BlockSpec((1, d), lambda i, f: (0, 0)),
            pl.BlockSpec((None, d, tf), lambda i, f: (layer, 0, f)),
            pl.BlockSpec((None, d, tf), lambda i, f: (layer, 0, nf + f)),
            pl.BlockSpec((None, tf, d), lambda i, f: (layer, f, 0)),
        ],
        out_specs=out_specs,
        scratch_shapes=scratch,
        compiler_params=_params("parallel", "arbitrary"),
        name="ffn",
    )(h, x, g_next.reshape(1, d), w_in, w_in, w_out)
    return res if emit_x else res[0]


def _proj_body(h_ref, w_ref, o_ref, *, act):
    acc = _dot_nt(h_ref[...], w_ref[0])
    if act == "sigmoid":
        acc = _sigmoid(acc)
    o_ref[...] = acc.astype(o_ref.dtype)


def proj(h, w_t, layer, first_row, n, *, act=None, tm=2048, tn=512):
    m, k = h.shape
    assert first_row % 8 == 0 and n % tn == 0
    w_spec = pl.BlockSpec((pl.Element(1), pl.Element(tn), pl.Element(k)),
                          lambda i, j: (layer, pl.multiple_of(first_row + j * tn, 8), 0))
    return pl.pallas_call(
        functools.partial(_proj_body, act=act),
        out_shape=jax.ShapeDtypeStruct((m, n), BF16),
        grid=(m // tm, n // tn),
        in_specs=[pl.BlockSpec((tm, k), lambda i, j: (i, 0)), w_spec],
        out_specs=pl.BlockSpec((tm, tn), lambda i, j: (i, j)),
        compiler_params=_params("parallel", "arbitrary"),
        name="proj",
    )(h, w_t)


def _proj_small_body(h_ref, wf_ref, wd_ref, o_ref):
    k = h_ref.shape[1]
    pad = jnp.zeros((LANES - D_HEADS - B_HEADS, k), F32)
    w = jnp.concatenate([wf_ref[...], wd_ref[...], pad], axis=0)
    o_ref[...] = _dot_nt(h_ref[...], w)


def proj_small(h, w_t, layer, tm=2048):
    m, k = h.shape
    return pl.pallas_call(
        _proj_small_body,
        out_shape=jax.ShapeDtypeStruct((m, LANES), F32),
        grid=(m // tm,),
        in_specs=[
            pl.BlockSpec((tm, k), lambda i: (i, 0)),
            pl.BlockSpec((None, D_HEADS, k), lambda i: (layer, COL_FL // D_HEADS, 0)),
            pl.BlockSpec((None, B_HEADS, k), lambda i: (layer, COL_DT // B_HEADS, 0)),
        ],
        out_specs=pl.BlockSpec((tm, LANES), lambda i: (i, 0)),
        compiler_params=_params("parallel"),
        name="proj_small",
    )(h, w_t, w_t)


def _sgu_body(p_ref, lng_ref, w_ref, b_ref, o_ref):
    causal = _lower_tri(CHUNK)
    for ci in range(p_ref.shape[0] // CHUNK):
        rows = slice(ci * CHUNK, (ci + 1) * CHUNK)
        p = jax.nn.gelu(p_ref[rows, :].astype(F32))
        u = p[:, :WIDTH]
        v = p[:, WIDTH:]
        mu = jnp.mean(v, axis=-1, keepdims=True)
        var = jnp.mean(jnp.square(v - mu), axis=-1, keepdims=True)
        vn = (v - mu) * lax.rsqrt(var + EPS) * lng_ref[...]
        for g in range(A_GROUPS):
            cols = slice(g * LANES, (g + 1) * LANES)
            w = jnp.where(causal, w_ref[g], 0.0)
            mixed = _dot(w, vn[:, cols]) + b_ref[:, cols]
            o_ref[rows, cols] = (u[:, cols] * mixed).astype(o_ref.dtype)


def sgu(p_ab, ln_g, w_s, b_full, layer, rows=4 * CHUNK):
    m = p_ab.shape[0]
    return pl.pallas_call(
        _sgu_body,
        out_shape=jax.ShapeDtypeStruct((m, WIDTH), BF16),
        grid=(m // rows,),
        in_specs=[
            pl.BlockSpec((rows, 2 * WIDTH), lambda c: (c, 0)),
            pl.BlockSpec((1, WIDTH), lambda c: (0, 0)),
            pl.BlockSpec((None, A_GROUPS, CHUNK, CHUNK), lambda c: (layer, 0, 0, 0)),
            pl.BlockSpec((CHUNK, WIDTH), lambda c: (0, 0)),
        ],
        out_specs=pl.BlockSpec((rows, WIDTH), lambda c: (c, 0)),
        compiler_params=_params("parallel"),
        name="sgu",
    )(p_ab, ln_g.reshape(1, WIDTH), w_s, b_full)


def _ssd_body(z_ref, xbc_ref, small_ref, cw_ref, cb_ref, dtb_ref, alog_ref, dsk_ref, ng_ref, o_ref,
              xpad_ref, state_ref, y_ref):
    c = pl.program_id(1)

    @pl.when(c == 0)
    def _():
        xpad_ref[0:8, :] = jnp.zeros((8, B_CONV_DIM), F32)
        state_ref[...] = jnp.zeros_like(state_ref)

    xpad_ref[8:8 + CHUNK, :] = xbc_ref[...].astype(F32)
    conv = cb_ref[...]
    for j in range(B_CONV):
        conv = conv + cw_ref[j:j + 1, :] * xpad_ref[pl.ds(8 - (B_CONV - 1) + j, CHUNK), :]
    xpad_ref[0:8, :] = xpad_ref[CHUNK:CHUNK + 8, :]
    xbc = _silu(conv)
    xs = xbc[:, :WIDTH]
    bm = xbc[:, WIDTH:WIDTH + B_GROUPS * B_STATE]
    cm = xbc[:, WIDTH + B_GROUPS * B_STATE:]

    dt = _softplus(small_ref[...] + dtb_ref[...])
    a = -jnp.exp(alog_ref[...])
    causal = _lower_tri(CHUNK)
    acum = _mask_dot_f32(causal, dt * a)
    acum_t = acum.T
    expand = (lax.broadcasted_iota(jnp.int32, (LANES, WIDTH), 0) - SMALL_DT
              == lax.broadcasted_iota(jnp.int32, (LANES, WIDTH), 1) // B_HEAD_DIM)
    dt_e = _dot_f32_by_mask(dt, expand)
    acum_e = _dot_f32_by_mask(acum, expand)
    eacum_e = jnp.exp(acum_e)
    dend_e = jnp.exp(acum_e[CHUNK - 1:CHUNK, :] - acum_e)
    xdt = xs * dt_e
    xend = xdt * dend_e
    lane = lax.broadcasted_iota(jnp.int32, (CHUNK, LANES), 1)
    low_half = lane < B_HEAD_DIM
    hpg = B_HEADS // B_GROUPS
    gw = WIDTH // B_GROUPS

    for g in range(B_GROUPS):
        bg = bm[:, g * B_STATE:(g + 1) * B_STATE]
        cg = cm[:, g * B_STATE:(g + 1) * B_STATE]
        cb = _dot_nt(cg, bg)
        gcols = slice(g * gw, (g + 1) * gw)
        prev = state_ref[:, gcols]
        y_off = _dot(cg, prev) * eacum_e[:, gcols]
        for pair in range(hpg // 2):
            pcols = slice(g * gw + pair * LANES, g * gw + (pair + 1) * LANES)
            x_pair = xdt[:, pcols]
            y_pair = jnp.zeros((CHUNK, LANES), F32)
            for half in range(2):
                hl = SMALL_DT + g * hpg + 2 * pair + half
                seg = acum[:, hl:hl + 1] - acum_t[hl:hl + 1, :]
                decay = jnp.exp(jnp.where(causal, seg, -jnp.inf))
                x_half = jnp.where(low_half if half == 0 else ~low_half, x_pair, 0.0)
                y_pair = y_pair + _dot(cb * decay, x_half)
            y_pair = y_pair + y_off[:, pair * LANES:(pair + 1) * LANES] + xs[:, pcols] * dsk_ref[:, pcols]
            y_ref[:, pcols] = y_pair
        st = _dot(bg.T, xend[:, gcols])
        state_ref[:, gcols] = prev * eacum_e[CHUNK - 1:CHUNK, gcols] + st

    y = y_ref[...] * _silu(z_ref[...].astype(F32))
    for g in range(B_GROUPS):
        gcols = slice(g * gw, (g + 1) * gw)
        yg = y[:, gcols]
        yg = yg * lax.rsqrt(jnp.mean(yg * yg, axis=-1, keepdims=True) + EPS)
        o_ref[:, gcols] = (yg * ng_ref[:, gcols]).astype(o_ref.dtype)


def ssd(p_ab, p_small, conv_w, conv_b, dtb_row, alog_row, dskip_row, norm_g, layer, bsz, seq):
    nc = seq // CHUNK
    m = bsz * seq
    row = lambda b, c: b * nc + c
    return pl.pallas_call(
        _ssd_body,
        out_shape=jax.ShapeDtypeStruct((m, WIDTH), BF16),
        grid=(bsz, nc),
        in_specs=[
            pl.BlockSpec((CHUNK, WIDTH), lambda b, c: (row(b, c), 2)),
            pl.BlockSpec((CHUNK, B_CONV_DIM), lambda b, c: (row(b, c), 2)),
            pl.BlockSpec((CHUNK, LANES), lambda b, c: (row(b, c), 0)),
            pl.BlockSpec((None, B_CONV, B_CONV_DIM), lambda b, c: (layer, 0, 0)),
            pl.BlockSpec((None, 1, B_CONV_DIM), lambda b, c: (layer, 0, 0)),
            pl.BlockSpec((1, LANES), lambda b, c: (0, 0)),
            pl.BlockSpec((1, LANES), lambda b, c: (0, 0)),
            pl.BlockSpec((1, WIDTH), lambda b, c: (0, 0)),
            pl.BlockSpec((1, WIDTH), lambda b, c: (0, 0)),
        ],
        out_specs=pl.BlockSpec((CHUNK, WIDTH), lambda b, c: (row(b, c), 0)),
        scratch_shapes=[pltpu.VMEM((CHUNK + 8, B_CONV_DIM), F32), pltpu.VMEM((B_STATE, WIDTH), F32),
                        pltpu.VMEM((CHUNK, WIDTH), F32)],
        compiler_params=_params("parallel", "arbitrary"),
        name="ssd",
    )(p_ab, p_ab, p_small, conv_w, conv_b.reshape(conv_b.shape[0], 1, B_CONV_DIM), dtb_row, alog_row,
      dskip_row, norm_g.reshape(1, WIDTH))


def _retention_tables(seq):
    half = C_HEAD_QK // 2
    inv_freq = 1.0 / (ROPE_BASE ** (jnp.arange(half, dtype=F32) / half))
    ang = jnp.arange(seq, dtype=F32)[:, None] * inv_freq[None, :]
    cos = jnp.tile(jnp.cos(ang), (1, 2 * LANES // C_HEAD_QK))
    sin = jnp.tile(jnp.concatenate([-jnp.sin(ang), jnp.sin(ang)], axis=1), (1, LANES // C_HEAD_QK))
    log_gamma = jnp.log(1.0 - 2.0 ** (-5.0 - jnp.arange(C_HEADS, dtype=F32)))
    idx = jnp.arange(CHUNK, dtype=F32)
    mask = idx[:, None] >= idx[None, :]
    rel = jnp.where(mask, idx[:, None] - idx[None, :], 0.0)
    intra = (jnp.exp(rel[..., None] * log_gamma) * mask[..., None]).transpose(2, 0, 1)
    k_decay = jnp.repeat(jnp.exp((CHUNK - 1.0 - idx)[:, None] * log_gamma), C_HEAD_QK, axis=1)
    q_decay = jnp.repeat(jnp.exp((idx + 1.0)[:, None] * log_gamma), C_HEAD_QK, axis=1)
    chunk_decay = jnp.broadcast_to(jnp.repeat(jnp.exp(CHUNK * log_gamma), C_HEAD_QK)[:, None],
                                   (C_QK, LANES))
    return cos, sin, intra, k_decay, q_decay, chunk_decay


def _retention_body(qk_ref, v_ref, g_ref, cos_ref, sin_ref, intra_ref, kdec_ref, qdec_ref, cdec_ref,
                    small_ref, fb_ref, o_ref, cum_ref, cum_t_ref, state_ref, carry_ref):
    @pl.when(pl.program_id(0) == 0)
    def _():
        state_ref[...] = jnp.zeros_like(state_ref)
        carry_ref[...] = jnp.zeros_like(carry_ref)

    for b in range(qk_ref.shape[0]):
        _retention_chunk(qk_ref.at[b], v_ref.at[b], g_ref.at[b], cos_ref, sin_ref, intra_ref, kdec_ref, qdec_ref,
                         cdec_ref, small_ref.at[b], fb_ref, o_ref.at[b], cum_ref.at[b], cum_t_ref.at[b],
                         state_ref.at[b], carry_ref.at[b])


def _retention_chunk(qk_ref, v_ref, g_ref, cos_ref, sin_ref, intra_ref, kdec_ref, qdec_ref, cdec_ref,
                     small_ref, fb_ref, o_ref, cum_ref, cum_t_ref, state_ref, carry_ref):
    log_f = -_softplus(-(small_ref[...] + fb_ref[...]))
    cum = _mask_dot_f32(_lower_tri(CHUNK), log_f) + carry_ref[...]
    carry_ref[...] = cum[CHUNK - 1:CHUNK, :]
    cum_ref[...] = cum
    cum_t_ref[...] = cum.T

    lane = lax.broadcasted_iota(jnp.int32, (CHUNK, LANES), 1)
    first_half = (lane % C_HEAD_QK) < (C_HEAD_QK // 2)
    low_head = lane < C_HEAD_QK
    low_rows = lax.broadcasted_iota(jnp.int32, (LANES, LANES), 0) < C_HEAD_QK
    cos = cos_ref[...]
    sin = sin_ref[...]

    def rotary(x):
        partner = jnp.where(first_half, pltpu.roll(x, LANES - C_HEAD_QK // 2, 1), pltpu.roll(x, C_HEAD_QK // 2, 1))
        return x * cos + partner * sin

    for pair in range(C_HEADS // 2):
        pc = slice(pair * LANES, (pair + 1) * LANES)
        q2 = rotary(qk_ref[:, pc].astype(F32))
        k2 = rotary(qk_ref[:, C_QK + pair * LANES:C_QK + (pair + 1) * LANES].astype(F32)) * (C_HEAD_QK ** -0.5)
        qd2 = q2 * qdec_ref[:, pc]
        kd2_t = (k2 * kdec_ref[:, pc]).T
        prev = state_ref[pc, :]
        new_state = []
        for half in range(2):
            h = 2 * pair + half
            hc = slice(h * LANES, (h + 1) * LANES)
            sel = low_head if half == 0 else ~low_head
            scores = _dot_nt(jnp.where(sel, q2, 0.0), k2) * intra_ref[h]
            vh = v_ref[:, hc]
            y = _dot(scores, vh) + _dot(jnp.where(sel, qd2, 0.0), prev)
            mu = jnp.mean(y, axis=-1, keepdims=True)
            var = jnp.mean(jnp.square(y - mu), axis=-1, keepdims=True)
            y = (y - mu) * lax.rsqrt(var + EPS)
            o_ref[:, hc] = (_silu(g_ref[:, hc].astype(F32)) * y).astype(o_ref.dtype)
            new_state.append(_dot(kd2_t, vh))
        state_ref[pc, :] = prev * cdec_ref[pc, :] + jnp.where(low_rows, new_state[0], new_state[1])


def retention(p_cd, p_small, fb_row, tables, bsz, seq):
    m = bsz * seq
    cos, sin, intra, k_decay, q_decay, chunk_decay = tables
    const2 = lambda c: (0, 0)
    p3 = p_cd.reshape(bsz, seq, p_cd.shape[1])
    y, cum, cum_t = pl.pallas_call(
        _retention_body,
        out_shape=[jax.ShapeDtypeStruct((bsz, seq, WIDTH), BF16), jax.ShapeDtypeStruct((bsz, seq, LANES), F32),
                   jax.ShapeDtypeStruct((bsz, LANES, seq), F32)],
        grid=(seq // CHUNK,),
        in_specs=[
            pl.BlockSpec((bsz, CHUNK, 2 * C_QK), lambda c: (0, c, 0)),
            pl.BlockSpec((bsz, CHUNK, WIDTH), lambda c: (0, c, 1)),
            pl.BlockSpec((bsz, CHUNK, WIDTH), lambda c: (0, c, 2)),
            pl.BlockSpec((CHUNK, LANES), lambda c: (c, 0)),
            pl.BlockSpec((CHUNK, LANES), lambda c: (c, 0)),
            pl.BlockSpec((C_HEADS, CHUNK, CHUNK), lambda c: (0, 0, 0)),
            pl.BlockSpec((CHUNK, C_QK), const2),
            pl.BlockSpec((CHUNK, C_QK), const2),
            pl.BlockSpec((C_QK, LANES), const2),
            pl.BlockSpec((bsz, CHUNK, LANES), lambda c: (0, c, 0)),
            pl.BlockSpec((1, LANES), const2),
        ],
        out_specs=[pl.BlockSpec((bsz, CHUNK, WIDTH), lambda c: (0, c, 0)),
                   pl.BlockSpec((bsz, CHUNK, LANES), lambda c: (0, c, 0)),
                   pl.BlockSpec((bsz, LANES, CHUNK), lambda c: (0, 0, c))],
        scratch_shapes=[pltpu.VMEM((bsz, C_QK, LANES), F32), pltpu.VMEM((bsz, 1, LANES), F32)],
        compiler_params=_params("arbitrary"),
        name="retention",
    )(p3, p3, p3, cos, sin, intra, k_decay, q_decay, chunk_decay, p_small.reshape(bsz, seq, LANES), fb_row)
    return y.reshape(m, WIDTH), cum.reshape(m, LANES), cum_t


def _fox_body(q_ref, k_ref, v_ref, cum_ref, cum_t_ref, o_ref, *, blk, hps):
    hg = pl.program_id(1)
    qi = pl.program_id(2)
    lane = lax.broadcasted_iota(jnp.int32, (blk, LANES), 1)
    scale = LOG2E * D_HEAD_DIM ** -0.5
    cols = [slice(i * D_HEAD_DIM, (i + 1) * D_HEAD_DIM) for i in range(hps)]
    q = [q_ref[:, c].astype(BF16) for c in cols]
    cq = [LOG2E * jnp.sum(jnp.where(lane == SMALL_FL + hg * hps + i, cum_ref[...], 0.0), axis=-1, keepdims=True)
          for i in range(hps)]

    def scores(i, kj):
        ks = pl.multiple_of(kj * blk, blk)
        s = _dot_nt(q[i], k_ref[pl.ds(ks, blk), cols[i]]) * scale
        return s + cq[i] - LOG2E * cum_t_ref[pl.ds(SMALL_FL + hg * hps + i, 1), pl.ds(ks, blk)], ks

    def update(i, carry, s, ks):
        m_prev, l_prev, acc = carry
        m_new = jnp.maximum(m_prev, jnp.max(s, axis=-1, keepdims=True))
        alpha = jnp.exp2(m_prev - m_new)
        p = jnp.exp2(s - m_new)
        l_new = alpha * l_prev + jnp.sum(p, axis=-1, keepdims=True)
        acc = alpha * acc + _dot(p, v_ref[pl.ds(ks, blk), cols[i]])
        return m_new, l_new, acc

    def full_block(kj, carries):
        return tuple(update(i, carries[i], *scores(i, kj)) for i in range(hps))

    init = (jnp.full((blk, 1), -jnp.inf, F32), jnp.zeros((blk, 1), F32), jnp.zeros((blk, D_HEAD_DIM), F32))
    carries = lax.fori_loop(0, qi, full_block, (init,) * hps)
    causal = _lower_tri(blk)
    for i in range(hps):
        s, ks = scores(i, qi)
        _, l_fin, acc = update(i, carries[i], jnp.where(causal, s, -jnp.inf), ks)
        o_ref[:, cols[i]] = (acc / l_fin).astype(o_ref.dtype)


def fox(p_cd, cum, cum_t, bsz, seq, blk=512, hps=4):
    nq = seq // blk
    width = hps * D_HEAD_DIM
    col0 = (2 * C_QK + 2 * WIDTH) // width
    ng = D_HEADS // hps
    return pl.pallas_call(
        functools.partial(_fox_body, blk=blk, hps=hps),
        out_shape=jax.ShapeDtypeStruct((bsz * seq, WIDTH), BF16),
        grid=(bsz, ng, nq),
        in_specs=[
            pl.BlockSpec((blk, width), lambda b, g, i: (b * nq + i, col0 + g)),
            pl.BlockSpec((seq, width), lambda b, g, i: (b, col0 + ng + g)),
            pl.BlockSpec((seq, width), lambda b, g, i: (b, col0 + 2 * ng + g)),
            pl.BlockSpec((blk, LANES), lambda b, g, i: (b * nq + i, 0)),
            pl.BlockSpec((None, 8, seq), lambda b, g, i: (b, SMALL_FL // 8, 0)),
        ],
        out_specs=pl.BlockSpec((blk, width), lambda b, g, i: (b * nq + i, g)),
        compiler_params=_params("parallel", "parallel", "arbitrary"),
        name="fox",
    )(p_cd, p_cd, p_cd, cum, cum_t)


def _merge_body(ya_ref, yb_ref, yc_ref, yd_ref, ga_ref, gb_ref, gc_ref, gd_ref, wb_ref, o_ref):
    merged = ga_ref[...].astype(F32) * _dot(ya_ref[...], wb_ref[0])
    merged = merged + gb_ref[...].astype(F32) * _dot(yb_ref[...], wb_ref[1])
    merged = merged + gc_ref[...].astype(F32) * _dot(yc_ref[...], wb_ref[2])
    merged = merged + gd_ref[...].astype(F32) * _dot(yd_ref[...], wb_ref[3])
    o_ref[...] = merged.astype(o_ref.dtype)


def merge(ys, gates, w_branch, layer, tm=1024, tn=512):
    m = ys[0].shape[0]
    d = w_branch.shape[-1]
    nj = d // tn
    y_spec = pl.BlockSpec((tm, WIDTH), lambda i, j: (i, 0))
    gate_specs = [pl.BlockSpec((tm, tn), functools.partial(lambda i, j, n: (i, n * nj + j), n=n)) for n in range(4)]
    return pl.pallas_call(
        _merge_body,
        out_shape=jax.ShapeDtypeStruct((m, d), BF16),
        grid=(m // tm, nj),
        in_specs=[y_spec] * 4 + gate_specs + [pl.BlockSpec((None, 4, WIDTH, tn), lambda i, j: (layer, 0, 0, j))],
        out_specs=pl.BlockSpec((tm, tn), lambda i, j: (i, j)),
        compiler_params=_params("parallel", "arbitrary"),
        name="merge",
    )(*ys, gates, gates, gates, gates, w_branch)


def _out_proj_body(a_ref, x_ref, gn_ref, w_ref, xo_ref, ho_ref):
    x_new = x_ref[...] + _dot(a_ref[...], w_ref[...])
    xo_ref[...] = x_new
    ho_ref[...] = _rmsnorm_rows(x_new, gn_ref[...]).astype(ho_ref.dtype)


def out_proj(a, x, w, layer, g_next, tm=512):
    m, d = x.shape
    k = a.shape[1]
    row_spec = pl.BlockSpec((tm, d), lambda i: (i, 0))
    return pl.pallas_call(
        _out_proj_body,
        out_shape=[jax.ShapeDtypeStruct((m, d), F32), jax.ShapeDtypeStruct((m, d), BF16)],
        grid=(m // tm,),
        in_specs=[
            pl.BlockSpec((tm, k), lambda i: (i, 0)),
            row_spec,
            pl.BlockSpec((1, d), lambda i: (0, 0)),
            pl.BlockSpec((None, k, d), lambda i: (layer, 0, 0), pipeline_mode=pl.Buffered(1)),
        ],
        out_specs=[row_spec, row_spec],
        compiler_params=_params("parallel"),
        name="out_proj",
    )(a, x, g_next.reshape(1, d), w)


def _pad_lanes(v, offset):
    return jnp.zeros((1, LANES), F32).at[0, offset:offset + v.shape[0]].set(v.astype(F32))


def kernel(x, ffn1_norm, ffn1_w_in, ffn1_w_out, mix_norm, w_mix_in, sgu_norm, sgu_w, sgu_b, conv_w, conv_b,
           dt_bias, a_log, d_skip, ssm_norm, forget_bias, w_branch, w_mix_out, ffn2_norm, ffn2_w_in, ffn2_w_out,
           final_norm):
    bsz, seq, d = x.shape
    depth = ffn1_norm.shape[0]
    m = bsz * seq
    x = x.reshape(m, d)
    tables = _retention_tables(seq)
    w_t = jnp.swapaxes(w_mix_in, 1, 2)
    h = rmsnorm(x, ffn1_norm[0], BF16)
    for l in range(depth):
        x, h = ffn(h, x, ffn1_w_in, ffn1_w_out, l, mix_norm[l], emit_x=True, h_dtype=BF16)

        p_ab = proj(h, w_t, l, 0, N_AB)
        p_small = proj_small(h, w_t, l)
        p_cd = proj(h, w_t, l, COL_C, N_CD)
        gates = proj(h, w_t, l, COL_G, 4 * d, act="sigmoid")

        b_full = jnp.repeat(sgu_b[l].T, LANES, axis=1)
        y_a = sgu(p_ab, sgu_norm[l], sgu_w, b_full, l)
        y_b = ssd(p_ab, p_small, conv_w, conv_b, _pad_lanes(dt_bias[l], SMALL_DT), _pad_lanes(a_log[l], SMALL_DT),
                  jnp.repeat(d_skip[l], B_HEAD_DIM).reshape(1, WIDTH), ssm_norm[l], l, bsz, seq)
        y_c, cum, cum_t = retention(p_cd, p_small, _pad_lanes(forget_bias[l], SMALL_FL), tables, bsz, seq)
        y_d = fox(p_cd, cum, cum_t, bsz, seq)

        merged = merge((y_a, y_b, y_c, y_d), gates, w_branch, l)
        x, h = out_proj(merged, x, w_mix_out, l, ffn2_norm[l])
        if l + 1 < depth:
            x, h = ffn(h, x, ffn2_w_in, ffn2_w_out, l, ffn1_norm[l + 1], emit_x=True, h_dtype=BF16)
        else:
            h = ffn(h, x, ffn2_w_in, ffn2_w_out, l, final_norm, emit_x=False, h_dtype=F32, tm=1024, tf=512)
    return h.reshape(bsz, seq, d)
```

```python
import functools

import jax
import jax.numpy as jnp
from jax import lax
from jax.experimental import pallas as pl
from jax.experimental.pallas import tpu as pltpu

F32 = jnp.float32
BF16 = jnp.bfloat16

EPS = 1e-6
LOG2E = 1.4426950408889634
CHUNK = 128
LANES = 128
VMEM_LIMIT = 56 * 1024 * 1024

D_MODEL = 2048
D_FF = 5632
WIDTH = 1024
A_GROUPS = 8
B_HEADS = 16
B_HEAD_DIM = 64
B_STATE = 128
B_GROUPS = 2
B_CONV = 4
B_CONV_DIM = WIDTH + 2 * B_GROUPS * B_STATE
C_HEADS = 8
C_HEAD_QK = 64
C_QK = C_HEADS * C_HEAD_QK
D_HEADS = 8
D_HEAD_DIM = 128
ROPE_BASE = 10000.0

COL_DT = 2 * WIDTH + WIDTH + B_CONV_DIM
COL_C = COL_DT + B_HEADS
COL_FL = COL_C + 2 * C_QK + 2 * WIDTH + 3 * WIDTH
COL_G = COL_FL + D_HEADS
N_AB = COL_DT
N_CD = COL_FL - COL_C
SMALL_FL = 0
SMALL_DT = D_HEADS


def _params(*sem):
    return pltpu.CompilerParams(dimension_semantics=sem, vmem_limit_bytes=VMEM_LIMIT)


def _dot(a, b):
    return jnp.dot(a.astype(BF16), b.astype(BF16), preferred_element_type=F32)


def _dot_nt(a, b):
    return lax.dot_general(a.astype(BF16), b.astype(BF16), (((1,), (1,)), ((), ())),
                           preferred_element_type=F32)


def _split3(x):
    hi = x.astype(BF16)
    rest = x - hi.astype(F32)
    mid = rest.astype(BF16)
    lo = (rest - mid.astype(F32)).astype(BF16)
    return lo, mid, hi


def _dot_f32_by_mask(x, mask01):
    mask01 = mask01.astype(BF16)
    lo, mid, hi = (jnp.dot(t, mask01, preferred_element_type=F32) for t in _split3(x))
    return lo + mid + hi


def _mask_dot_f32(mask01, x):
    mask01 = mask01.astype(BF16)
    lo, mid, hi = (jnp.dot(mask01, t, preferred_element_type=F32) for t in _split3(x))
    return lo + mid + hi


def _sigmoid(x):
    return 0.5 * jnp.tanh(0.5 * x) + 0.5


def _silu(x):
    return x * _sigmoid(x)


def _softplus(x):
    return jnp.maximum(x, 0.0) + jnp.log1p(jnp.exp(-jnp.abs(x)))


def _lower_tri(n):
    return lax.broadcasted_iota(jnp.int32, (n, n), 1) <= lax.broadcasted_iota(jnp.int32, (n, n), 0)


def _rmsnorm_rows(x, g):
    return x * lax.rsqrt(jnp.mean(x * x, axis=-1, keepdims=True) + EPS) * g


def _rmsnorm_body(x_ref, g_ref, o_ref):
    o_ref[...] = _rmsnorm_rows(x_ref[...], g_ref[...]).astype(o_ref.dtype)


def rmsnorm(x, g, out_dtype, tm=512):
    m, d = x.shape
    return pl.pallas_call(
        _rmsnorm_body,
        out_shape=jax.ShapeDtypeStruct((m, d), out_dtype),
        grid=(m // tm,),
        in_specs=[pl.BlockSpec((tm, d), lambda i: (i, 0)), pl.BlockSpec((1, d), lambda i: (0, 0))],
        out_specs=pl.BlockSpec((tm, d), lambda i: (i, 0)),
        compiler_params=_params("parallel"),
        name="rmsnorm",
    )(x, g.reshape(1, d))


def _residual_rows_copy(x_hbm, acc_ref, sem, tm):
    rows = pl.ds(pl.multiple_of(pl.program_id(0) * tm, tm), tm)
    return pltpu.make_async_copy(x_hbm.at[rows, :], acc_ref, sem)


def _ffn_body(h_ref, x_hbm, gn_ref, wg_ref, wu_ref, wo_ref, *refs, nf, tm, emit_x):
    xo_ref, ho_ref, sem = refs if emit_x else (refs[1], refs[0], refs[2])
    f = pl.program_id(1)

    @pl.when(f == 0)
    def _():
        copy = _residual_rows_copy(x_hbm, xo_ref, sem, tm)
        copy.start()
        copy.wait()

    tf = wg_ref.shape[1]
    gu = _dot(h_ref[...], jnp.concatenate([wg_ref[...].astype(BF16), wu_ref[...].astype(BF16)], axis=1))
    xo_ref[...] += _dot(0.5 * _silu(gu[:, :tf]) * gu[:, tf:], wo_ref[...])

    @pl.when(f == nf - 1)
    def _():
        ho_ref[...] = _rmsnorm_rows(xo_ref[...], gn_ref[...]).astype(ho_ref.dtype)


def ffn(h, x, w_in, w_out, layer, g_next, *, emit_x, h_dtype, tm=1024, tf=512):
    m, d = x.shape
    nf = D_FF // tf
    row_spec = pl.BlockSpec((tm, d), lambda i, f: (i, 0), pipeline_mode=pl.Buffered(1))
    out_shape = [jax.ShapeDtypeStruct((m, d), h_dtype)]
    out_specs = [row_spec]
    scratch = [pltpu.VMEM((tm, d), F32), pltpu.SemaphoreType.DMA(())]
    if emit_x:
        out_shape.insert(0, jax.ShapeDtypeStruct((m, d), F32))
        out_specs.insert(0, row_spec)
        scratch = scratch[1:]
    res = pl.pallas_call(
        functools.partial(_ffn_body, nf=nf, tm=tm, emit_x=emit_x),
        out_shape=out_shape,
        grid=(m // tm, nf),
        in_specs=[
            row_spec,
            pl.BlockSpec(memory_space=pl.ANY),
            pl.BlockSpec((1, d), lambda i, f: (0, 0)),
            pl.BlockSpec((None, d, tf), lambda i, f: (layer, 0, f)),
            pl.BlockSpec((None, d, tf), lambda i, f: (layer, 0, nf + f)),
            pl.BlockSpec((None, tf, d), lambda i, f: (layer, f, 0)),
        ],
        out_specs=out_specs,
        scratch_shapes=scratch,
        compiler_params=_params("parallel", "arbitrary"),
        name="ffn",
    )(h, x, g_next.reshape(1, d), w_in, w_in, w_out)
    return res if emit_x else res[0]


def _proj_body(h_ref, w_ref, o_ref, *, act):
    acc = _dot_nt(h_ref[...], w_ref[0])
    if act == "sigmoid":
        acc = _sigmoid(acc)
    o_ref[...] = acc.astype(o_ref.dtype)


def proj(h, w_t, layer, first_row, n, *, act=None, tm=2048, tn=512):
    m, k = h.shape
    assert first_row % 8 == 0 and n % tn == 0
    w_spec = pl.BlockSpec((pl.Element(1), pl.Element(tn), pl.Element(k)),
                          lambda i, j: (layer, pl.multiple_of(first_row + j * tn, 8), 0))
    return pl.pallas_call(
        functools.partial(_proj_body, act=act),
        out_shape=jax.ShapeDtypeStruct((m, n), BF16),
        grid=(m // tm, n // tn),
        in_specs=[pl.BlockSpec((tm, k), lambda i, j: (i, 0)), w_spec],
        out_specs=pl.BlockSpec((tm, tn), lambda i, j: (i, j)),
        compiler_params=_params("parallel", "arbitrary"),
        name="proj",
    )(h, w_t)


def _proj_small_body(h_ref, wf_ref, wd_ref, o_ref):
    k = h_ref.shape[1]
    pad = jnp.zeros((LANES - D_HEADS - B_HEADS, k), F32)
    w = jnp.concatenate([wf_ref[...], wd_ref[...], pad], axis=0)
    o_ref[...] = _dot_nt(h_ref[...], w)


def proj_small(h, w_t, layer, tm=2048):
    m, k = h.shape
    return pl.pallas_call(
        _proj_small_body,
        out_shape=jax.ShapeDtypeStruct((m, LANES), F32),
        grid=(m // tm,),
        in_specs=[
            pl.BlockSpec((tm, k), lambda i: (i, 0)),
            pl.BlockSpec((None, D_HEADS, k), lambda i: (layer, COL_FL // D_HEADS, 0)),
            pl.BlockSpec((None, B_HEADS, k), lambda i: (layer, COL_DT // B_HEADS, 0)),
        ],
        out_specs=pl.BlockSpec((tm, LANES), lambda i: (i, 0)),
        compiler_params=_params("parallel"),
        name="proj_small",
    )(h, w_t, w_t)


def _sgu_body(p_ref, lng_ref, w_ref, b_ref, o_ref):
    causal = _lower_tri(CHUNK)
    for ci in range(p_ref.shape[0] // CHUNK):
        rows = slice(ci * CHUNK, (ci + 1) * CHUNK)
        p = jax.nn.gelu(p_ref[rows, :].astype(F32))
        u = p[:, :WIDTH]
        v = p[:, WIDTH:]
        mu = jnp.mean(v, axis=-1, keepdims=True)
        var = jnp.mean(jnp.square(v - mu), axis=-1, keepdims=True)
        vn = (v - mu) * lax.rsqrt(var + EPS) * lng_ref[...]
        for g in range(A_GROUPS):
            cols = slice(g * LANES, (g + 1) * LANES)
            w = jnp.where(causal, w_ref[g], 0.0)
            mixed = _dot(w, vn[:, cols]) + b_ref[:, cols]
            o_ref[rows, cols] = (u[:, cols] * mixed).astype(o_ref.dtype)


def sgu(p_ab, ln_g, w_s, b_full, layer, rows=4 * CHUNK):
    m = p_ab.shape[0]
    return pl.pallas_call(
        _sgu_body,
        out_shape=jax.ShapeDtypeStruct((m, WIDTH), BF16),
        grid=(m // rows,),
        in_specs=[
            pl.BlockSpec((rows, 2 * WIDTH), lambda c: (c, 0)),
            pl.BlockSpec((1, WIDTH), lambda c: (0, 0)),
            pl.BlockSpec((None, A_GROUPS, CHUNK, CHUNK), lambda c: (layer, 0, 0, 0)),
            pl.BlockSpec((CHUNK, WIDTH), lambda c: (0, 0)),
        ],
        out_specs=pl.BlockSpec((rows, WIDTH), lambda c: (c, 0)),
        compiler_params=_params("parallel"),
        name="sgu",
    )(p_ab, ln_g.reshape(1, WIDTH), w_s, b_full)


def _ssd_body(z_ref, xbc_ref, small_ref, cw_ref, cb_ref, dtb_ref, alog_ref, dsk_ref, ng_ref, o_ref,
              xprev_ref, state_ref, y_ref):
    @pl.when(pl.program_id(0) == 0)
    def _():
        xprev_ref[...] = jnp.zeros_like(xprev_ref)
        state_ref[...] = jnp.zeros_like(state_ref)

    for b in range(z_ref.shape[0]):
        _ssd_chunk(z_ref.at[b], xbc_ref.at[b], small_ref.at[b], cw_ref, cb_ref, dtb_ref, alog_ref, dsk_ref, ng_ref,
                   o_ref.at[b], xprev_ref.at[b], state_ref.at[b], y_ref.at[b])


def _ssd_chunk(z_ref, xbc_ref, small_ref, cw_ref, cb_ref, dtb_ref, alog_ref, dsk_ref, ng_ref, o_ref,
               xprev_ref, state_ref, y_ref):
    xprev_ref[8:8 + CHUNK, :] = xbc_ref[...].astype(F32)
    conv = cb_ref[...]
    for j in range(B_CONV):
        conv = conv + cw_ref[j:j + 1, :] * xprev_ref[pl.ds(8 - (B_CONV - 1) + j, CHUNK), :]
    xprev_ref[0:8, :] = xprev_ref[CHUNK:CHUNK + 8, :]
    xbc = _silu(conv)
    xs = xbc[:, :WIDTH]
    bm = xbc[:, WIDTH:WIDTH + B_GROUPS * B_STATE]
    cm = xbc[:, WIDTH + B_GROUPS * B_STATE:]

    dt = _softplus(small_ref[...] + dtb_ref[...])
    a = -jnp.exp(alog_ref[...])
    causal = _lower_tri(CHUNK)
    acum = _mask_dot_f32(causal, dt * a)
    acum_t = acum.T
    expand = (lax.broadcasted_iota(jnp.int32, (LANES, WIDTH), 0) - SMALL_DT
              == lax.broadcasted_iota(jnp.int32, (LANES, WIDTH), 1) // B_HEAD_DIM)
    dt_e = _dot_f32_by_mask(dt, expand)
    acum_e = _dot_f32_by_mask(acum, expand)
    eacum_e = jnp.exp(acum_e)
    dend_e = jnp.exp(acum_e[CHUNK - 1:CHUNK, :] - acum_e)
    xdt = xs * dt_e
    xend = xdt * dend_e
    lane = lax.broadcasted_iota(jnp.int32, (CHUNK, LANES), 1)
    low_half = lane < B_HEAD_DIM
    hpg = B_HEADS // B_GROUPS
    gw = WIDTH // B_GROUPS

    for g in range(B_GROUPS):
        bg = bm[:, g * B_STATE:(g + 1) * B_STATE]
        cg = cm[:, g * B_STATE:(g + 1) * B_STATE]
        cb = _dot_nt(cg, bg)
        gcols = slice(g * gw, (g + 1) * gw)
        prev = state_ref[:, gcols]
        y_off = _dot(cg, prev) * eacum_e[:, gcols]
        for pair in range(hpg // 2):
            pcols = slice(g * gw + pair * LANES, g * gw + (pair + 1) * LANES)
            x_pair = xdt[:, pcols]
            y_pair = jnp.zeros((CHUNK, LANES), F32)
            for half in range(2):
                hl = SMALL_DT + g * hpg + 2 * pair + half
                seg = acum[:, hl:hl + 1] - acum_t[hl:hl + 1, :]
                decay = jnp.exp(jnp.where(causal, seg, -jnp.inf))
                x_half = jnp.where(low_half if half == 0 else ~low_half, x_pair, 0.0)
                y_pair = y_pair + _dot(cb * decay, x_half)
            y_pair = y_pair + y_off[:, pair * LANES:(pair + 1) * LANES] + xs[:, pcols] * dsk_ref[:, pcols]
            y_ref[:, pcols] = y_pair
        st = _dot(bg.T, xend[:, gcols])
        state_ref[:, gcols] = prev * eacum_e[CHUNK - 1:CHUNK, gcols] + st

    y = y_ref[...] * _silu(z_ref[...].astype(F32))
    for g in range(B_GROUPS):
        gcols = slice(g * gw, (g + 1) * gw)
        yg = y[:, gcols]
        yg = yg * lax.rsqrt(jnp.mean(yg * yg, axis=-1, keepdims=True) + EPS)
        o_ref[:, gcols] = (yg * ng_ref[:, gcols]).astype(o_ref.dtype)


def ssd(p_ab, p_small, conv_w, conv_b, dtb_row, alog_row, dskip_row, norm_g, layer, bsz, seq):
    m = bsz * seq
    p3 = p_ab.reshape(bsz, seq, p_ab.shape[1])
    const2 = lambda c: (0, 0)
    y = pl.pallas_call(
        _ssd_body,
        out_shape=jax.ShapeDtypeStruct((bsz, seq, WIDTH), BF16),
        grid=(seq // CHUNK,),
        in_specs=[
            pl.BlockSpec((bsz, CHUNK, WIDTH), lambda c: (0, c, 2)),
            pl.BlockSpec((bsz, CHUNK, B_CONV_DIM), lambda c: (0, c, 2)),
            pl.BlockSpec((bsz, CHUNK, LANES), lambda c: (0, c, 0)),
            pl.BlockSpec((None, B_CONV, B_CONV_DIM), lambda c: (layer, 0, 0)),
            pl.BlockSpec((None, 1, B_CONV_DIM), lambda c: (layer, 0, 0)),
            pl.BlockSpec((1, LANES), const2),
            pl.BlockSpec((1, LANES), const2),
            pl.BlockSpec((1, WIDTH), const2),
            pl.BlockSpec((1, WIDTH), const2),
        ],
        out_specs=pl.BlockSpec((bsz, CHUNK, WIDTH), lambda c: (0, c, 0)),
        scratch_shapes=[pltpu.VMEM((bsz, CHUNK + 8, B_CONV_DIM), F32), pltpu.VMEM((bsz, B_STATE, WIDTH), F32),
                        pltpu.VMEM((bsz, CHUNK, WIDTH), F32)],
        compiler_params=_params("arbitrary"),
        name="ssd",
    )(p3, p3, p_small.reshape(bsz, seq, LANES), conv_w, conv_b.reshape(conv_b.shape[0], 1, B_CONV_DIM), dtb_row,
      alog_row, dskip_row, norm_g.reshape(1, WIDTH))
    return y.reshape(m, WIDTH)


def _retention_tables(seq):
    half = C_HEAD_QK // 2
    inv_freq = 1.0 / (ROPE_BASE ** (jnp.arange(half, dtype=F32) / half))
    ang = jnp.arange(seq, dtype=F32)[:, None] * inv_freq[None, :]
    cos = jnp.tile(jnp.cos(ang), (1, 2 * LANES // C_HEAD_QK))
    sin = jnp.tile(jnp.concatenate([-jnp.sin(ang), jnp.sin(ang)], axis=1), (1, LANES // C_HEAD_QK))
    log_gamma = jnp.log(1.0 - 2.0 ** (-5.0 - jnp.arange(C_HEADS, dtype=F32)))
    idx = jnp.arange(CHUNK, dtype=F32)
    mask = idx[:, None] >= idx[None, :]
    rel = jnp.where(mask, idx[:, None] - idx[None, :], 0.0)
    intra = (jnp.exp(rel[..., None] * log_gamma) * mask[..., None]).transpose(2, 0, 1)
    k_decay = jnp.repeat(jnp.exp((CHUNK - 1.0 - idx)[:, None] * log_gamma), C_HEAD_QK, axis=1)
    q_decay = jnp.repeat(jnp.exp((idx + 1.0)[:, None] * log_gamma), C_HEAD_QK, axis=1)
    chunk_decay = jnp.broadcast_to(jnp.repeat(jnp.exp(CHUNK * log_gamma), C_HEAD_QK)[:, None],
                                   (C_QK, LANES))
    return cos, sin, intra, k_decay, q_decay, chunk_decay


def _retention_body(qk_ref, v_ref, g_ref, cos_ref, sin_ref, intra_ref, kdec_ref, qdec_ref, cdec_ref,
                    small_ref, fb_ref, o_ref, cum_ref, cum_t_ref, state_ref, carry_ref):
    @pl.when(pl.program_id(0) == 0)
    def _():
        state_ref[...] = jnp.zeros_like(state_ref)
        carry_ref[...] = jnp.zeros_like(carry_ref)

    for b in range(qk_ref.shape[0]):
        _retention_chunk(qk_ref.at[b], v_ref.at[b], g_ref.at[b], cos_ref, sin_ref, intra_ref, kdec_ref, qdec_ref,
                         cdec_ref, small_ref.at[b], fb_ref, o_ref.at[b], cum_ref.at[b], cum_t_ref.at[b],
                         state_ref.at[b], carry_ref.at[b])


def _retention_chunk(qk_ref, v_ref, g_ref, cos_ref, sin_ref, intra_ref, kdec_ref, qdec_ref, cdec_ref,
                     small_ref, fb_ref, o_ref, cum_ref, cum_t_ref, state_ref, carry_ref):
    log_f = -_softplus(-(small_ref[...] + fb_ref[...]))
    cum = _mask_dot_f32(_lower_tri(CHUNK), log_f) + carry_ref[...]
    carry_ref[...] = cum[CHUNK - 1:CHUNK, :]
    cum_ref[...] = cum
    cum_t_ref[...] = cum.T

    lane = lax.broadcasted_iota(jnp.int32, (CHUNK, LANES), 1)
    first_half = (lane % C_HEAD_QK) < (C_HEAD_QK // 2)
    low_head = lane < C_HEAD_QK
    low_rows = lax.broadcasted_iota(jnp.int32, (LANES, LANES), 0) < C_HEAD_QK
    cos = cos_ref[...]
    sin = sin_ref[...]

    def rotary(x):
        partner = jnp.where(first_half, pltpu.roll(x, LANES - C_HEAD_QK // 2, 1), pltpu.roll(x, C_HEAD_QK // 2, 1))
        return x * cos + partner * sin

    for pair in range(C_HEADS // 2):
        pc = slice(pair * LANES, (pair + 1) * LANES)
        q2 = rotary(qk_ref[:, pc].astype(F32))
        k2 = rotary(qk_ref[:, C_QK + pair * LANES:C_QK + (pair + 1) * LANES].astype(F32)) * (C_HEAD_QK ** -0.5)
        qd2 = q2 * qdec_ref[:, pc]
        kd2_t = (k2 * kdec_ref[:, pc]).T
        prev = state_ref[pc, :]
        new_state = []
        for half in range(2):
            h = 2 * pair + half
            hc = slice(h * LANES, (h + 1) * LANES)
            sel = low_head if half == 0 else ~low_head
            scores = _dot_nt(jnp.where(sel, q2, 0.0), k2) * intra_ref[h]
            vh = v_ref[:, hc]
            y = _dot(scores, vh) + _dot(jnp.where(sel, qd2, 0.0), prev)
            mu = jnp.mean(y, axis=-1, keepdims=True)
            var = jnp.mean(jnp.square(y - mu), axis=-1, keepdims=True)
            y = (y - mu) * lax.rsqrt(var + EPS)
            o_ref[:, hc] = (_silu(g_ref[:, hc].astype(F32)) * y).astype(o_ref.dtype)
            new_state.append(_dot(kd2_t, vh))
        state_ref[pc, :] = prev * cdec_ref[pc, :] + jnp.where(low_rows, new_state[0], new_state[1])


def retention(p_cd, p_small, fb_row, tables, bsz, seq):
    m = bsz * seq
    cos, sin, intra, k_decay, q_decay, chunk_decay = tables
    const2 = lambda c: (0, 0)
    p3 = p_cd.reshape(bsz, seq, p_cd.shape[1])
    y, cum, cum_t = pl.pallas_call(
        _retention_body,
        out_shape=[jax.ShapeDtypeStruct((bsz, seq, WIDTH), BF16), jax.ShapeDtypeStruct((bsz, seq, LANES), F32),
                   jax.ShapeDtypeStruct((bsz, LANES, seq), F32)],
        grid=(seq // CHUNK,),
        in_specs=[
            pl.BlockSpec((bsz, CHUNK, 2 * C_QK), lambda c: (0, c, 0)),
            pl.BlockSpec((bsz, CHUNK, WIDTH), lambda c: (0, c, 1)),
            pl.BlockSpec((bsz, CHUNK, WIDTH), lambda c: (0, c, 2)),
            pl.BlockSpec((CHUNK, LANES), lambda c: (c, 0)),
            pl.BlockSpec((CHUNK, LANES), lambda c: (c, 0)),
            pl.BlockSpec((C_HEADS, CHUNK, CHUNK), lambda c: (0, 0, 0)),
            pl.BlockSpec((CHUNK, C_QK), const2),
            pl.BlockSpec((CHUNK, C_QK), const2),
            pl.BlockSpec((C_QK, LANES), const2),
            pl.BlockSpec((bsz, CHUNK, LANES), lambda c: (0, c, 0)),
            pl.BlockSpec((1, LANES), const2),
        ],
        out_specs=[pl.BlockSpec((bsz, CHUNK, WIDTH), lambda c: (0, c, 0)),
                   pl.BlockSpec((bsz, CHUNK, LANES), lambda c: (0, c, 0)),
                   pl.BlockSpec((bsz, LANES, CHUNK), lambda c: (0, 0, c))],
        scratch_shapes=[pltpu.VMEM((bsz, C_QK, LANES), F32), pltpu.VMEM((bsz, 1, LANES), F32)],
        compiler_params=_params("arbitrary"),
        name="retention",
    )(p3, p3, p3, cos, sin, intra, k_decay, q_decay, chunk_decay, p_small.reshape(bsz, seq, LANES), fb_row)
    return y.reshape(m, WIDTH), cum.reshape(m, LANES), cum_t


def _fox_body(q_ref, k_ref, v_ref, cum_ref, cum_t_ref, o_ref, *, blk, hps):
    hg = pl.program_id(1)
    qi = pl.program_id(2)
    lane = lax.broadcasted_iota(jnp.int32, (blk, LANES), 1)
    scale = LOG2E * D_HEAD_DIM ** -0.5
    cols = [slice(i * D_HEAD_DIM, (i + 1) * D_HEAD_DIM) for i in range(hps)]
    q = [q_ref[:, c].astype(BF16) for c in cols]
    cq = [LOG2E * jnp.sum(jnp.where(lane == SMALL_FL + hg * hps + i, cum_ref[...], 0.0), axis=-1, keepdims=True)
          for i in range(hps)]

    def scores(i, kj):
        ks = pl.multiple_of(kj * blk, blk)
        s = _dot_nt(q[i], k_ref[pl.ds(ks, blk), cols[i]]) * scale
        return s + cq[i] - LOG2E * cum_t_ref[pl.ds(SMALL_FL + hg * hps + i, 1), pl.ds(ks, blk)], ks

    def update(i, carry, s, ks):
        m_prev, l_prev, acc = carry
        m_new = jnp.maximum(m_prev, jnp.max(s, axis=-1, keepdims=True))
        alpha = jnp.exp2(m_prev - m_new)
        p = jnp.exp2(s - m_new)
        l_new = alpha * l_prev + jnp.sum(p, axis=-1, keepdims=True)
        acc = alpha * acc + _dot(p, v_ref[pl.ds(ks, blk), cols[i]])
        return m_new, l_new, acc

    def full_block(kj, carries):
        return tuple(update(i, carries[i], *scores(i, kj)) for i in range(hps))

    init = (jnp.full((blk, 1), -jnp.inf, F32), jnp.zeros((blk, 1), F32), jnp.zeros((blk, D_HEAD_DIM), F32))
    carries = lax.fori_loop(0, qi, full_block, (init,) * hps)
    causal = _lower_tri(blk)
    for i in range(hps):
        s, ks = scores(i, qi)
        _, l_fin, acc = update(i, carries[i], jnp.where(causal, s, -jnp.inf), ks)
        o_ref[:, cols[i]] = (acc / l_fin).astype(o_ref.dtype)


def fox(p_cd, cum, cum_t, bsz, seq, blk=512, hps=4):
    nq = seq // blk
    width = hps * D_HEAD_DIM
    col0 = (2 * C_QK + 2 * WIDTH) // width
    ng = D_HEADS // hps
    return pl.pallas_call(
        functools.partial(_fox_body, blk=blk, hps=hps),
        out_shape=jax.ShapeDtypeStruct((bsz * seq, WIDTH), BF16),
        grid=(bsz, ng, nq),
        in_specs=[
            pl.BlockSpec((blk, width), lambda b, g, i: (b * nq + i, col0 + g)),
            pl.BlockSpec((seq, width), lambda b, g, i: (b, col0 + ng + g)),
            pl.BlockSpec((seq, width), lambda b, g, i: (b, col0 + 2 * ng + g)),
            pl.BlockSpec((blk, LANES), lambda b, g, i: (b * nq + i, 0)),
            pl.BlockSpec((None, 8, seq), lambda b, g, i: (b, SMALL_FL // 8, 0)),
        ],
        out_specs=pl.BlockSpec((blk, width), lambda b, g, i: (b * nq + i, g)),
        compiler_params=_params("parallel", "parallel", "arbitrary"),
        name="fox",
    )(p_cd, p_cd, p_cd, cum, cum_t)


def _merge_body(ya_ref, yb_ref, yc_ref, yd_ref, ga_ref, gb_ref, gc_ref, gd_ref, wb_ref, o_ref):
    merged = ga_ref[...].astype(F32) * _dot(ya_ref[...], wb_ref[0])
    merged = merged + gb_ref[...].astype(F32) * _dot(yb_ref[...], wb_ref[1])
    merged = merged + gc_ref[...].astype(F32) * _dot(yc_ref[...], wb_ref[2])
    merged = merged + gd_ref[...].astype(F32) * _dot(yd_ref[...], wb_ref[3])
    o_ref[...] = merged.astype(o_ref.dtype)


def merge(ys, gates, w_branch, layer, tm=1024, tn=512):
    m = ys[0].shape[0]
    d = w_branch.shape[-1]
    nj = d // tn
    y_spec = pl.BlockSpec((tm, WIDTH), lambda i, j: (i, 0))
    gate_specs = [pl.BlockSpec((tm, tn), functools.partial(lambda i, j, n: (i, n * nj + j), n=n)) for n in range(4)]
    return pl.pallas_call(
        _merge_body,
        out_shape=jax.ShapeDtypeStruct((m, d), BF16),
        grid=(m // tm, nj),
        in_specs=[y_spec] * 4 + gate_specs + [pl.BlockSpec((None, 4, WIDTH, tn), lambda i, j: (layer, 0, 0, j))],
        out_specs=pl.BlockSpec((tm, tn), lambda i, j: (i, j)),
        compiler_params=_params("parallel", "arbitrary"),
        name="merge",
    )(*ys, gates, gates, gates, gates, w_branch)


def _out_proj_body(a_ref, x_ref, gn_ref, w_ref, xo_ref, ho_ref):
    x_new = x_ref[...] + _dot(a_ref[...], w_ref[...])
    xo_ref[...] = x_new
    ho_ref[...] = _rmsnorm_rows(x_new, gn_ref[...]).astype(ho_ref.dtype)


def out_proj(a, x, w, layer, g_next, tm=512):
    m, d = x.shape
    k = a.shape[1]
    row_spec = pl.BlockSpec((tm, d), lambda i: (i, 0))
    return pl.pallas_call(
        _out_proj_body,
        out_shape=[jax.ShapeDtypeStruct((m, d), F32), jax.ShapeDtypeStruct((m, d), BF16)],
        grid=(m // tm,),
        in_specs=[
            pl.BlockSpec((tm, k), lambda i: (i, 0)),
            row_spec,
            pl.BlockSpec((1, d), lambda i: (0, 0)),
            pl.BlockSpec((None, k, d), lambda i: (layer, 0, 0), pipeline_mode=pl.Buffered(1)),
        ],
        out_specs=[row_spec, row_spec],
        compiler_params=_params("parallel"),
        name="out_proj",
    )(a, x, g_next.reshape(1, d), w)


def _pad_lanes(v, offset):
    return jnp.zeros((1, LANES), F32).at[0, offset:offset + v.shape[0]].set(v.astype(F32))


def kernel(x, ffn1_norm, ffn1_w_in, ffn1_w_out, mix_norm, w_mix_in, sgu_norm, sgu_w, sgu_b, conv_w, conv_b,
           dt_bias, a_log, d_skip, ssm_norm, forget_bias, w_branch, w_mix_out, ffn2_norm, ffn2_w_in, ffn2_w_out,
           final_norm):
    bsz, seq, d = x.shape
    depth = ffn1_norm.shape[0]
    m = bsz * seq
    x = x.reshape(m, d)
    tables = _retention_tables(seq)
    w_t = jnp.swapaxes(w_mix_in, 1, 2)
    h = rmsnorm(x, ffn1_norm[0], BF16)
    for l in range(depth):
        x, h = ffn(h, x, ffn1_w_in, ffn1_w_out, l, mix_norm[l], emit_x=True, h_dtype=BF16)

        p_ab = proj(h, w_t, l, 0, N_AB)
        p_small = proj_small(h, w_t, l)
        p_cd = proj(h, w_t, l, COL_C, N_CD)
        gates = proj(h, w_t, l, COL_G, 4 * d, act="sigmoid")

        b_full = jnp.repeat(sgu_b[l].T, LANES, axis=1)
        y_a = sgu(p_ab, sgu_norm[l], sgu_w, b_full, l)
        y_b = ssd(p_ab, p_small, conv_w, conv_b, _pad_lanes(dt_bias[l], SMALL_DT), _pad_lanes(a_log[l], SMALL_DT),
                  jnp.repeat(d_skip[l], B_HEAD_DIM).reshape(1, WIDTH), ssm_norm[l], l, bsz, seq)
        y_c, cum, cum_t = retention(p_cd, p_small, _pad_lanes(forget_bias[l], SMALL_FL), tables, bsz, seq)
        y_d = fox(p_cd, cum, cum_t, bsz, seq)

        merged = merge((y_a, y_b, y_c, y_d), gates, w_branch, l)
        x, h = out_proj(merged, x, w_mix_out, l, ffn2_norm[l])
        if l + 1 < depth:
            x, h = ffn(h, x, ffn2_w_in, ffn2_w_out, l, ffn1_norm[l + 1], emit_x=True, h_dtype=BF16)
        else:
            h = ffn(h, x, ffn2_w_in, ffn2_w_out, l, final_norm, emit_x=False, h_dtype=F32)
    return h.reshape(bsz, seq, d)
```

```python
import functools

import jax
import jax.numpy as jnp
from jax import lax
from jax.experimental import pallas as pl
from jax.experimental.pallas import tpu as pltpu

F32 = jnp.float32
BF16 = jnp.bfloat16

EPS = 1e-6
LOG2E = 1.4426950408889634
CHUNK = 128
LANES = 128
VMEM_LIMIT = 56 * 1024 * 1024

D_MODEL = 2048
D_FF = 5632
WIDTH = 1024
A_GROUPS = 8
B_HEADS = 16
B_HEAD_DIM = 64
B_STATE = 128
B_GROUPS = 2
B_CONV = 4
B_CONV_DIM = WIDTH + 2 * B_GROUPS * B_STATE
C_HEADS = 8
C_HEAD_QK = 64
C_QK = C_HEADS * C_HEAD_QK
D_HEADS = 8
D_HEAD_DIM = 128
ROPE_BASE = 10000.0

COL_DT = 2 * WIDTH + WIDTH + B_CONV_DIM
COL_C = COL_DT + B_HEADS
COL_FL = COL_C + 2 * C_QK + 2 * WIDTH + 3 * WIDTH
COL_G = COL_FL + D_HEADS
N_AB = COL_DT
N_CD = COL_FL - COL_C
SMALL_FL = 0
SMALL_DT = D_HEADS


def _params(*sem):
    return pltpu.CompilerParams(dimension_semantics=sem, vmem_limit_bytes=VMEM_LIMIT)


def _dot(a, b):
    return jnp.dot(a.astype(BF16), b.astype(BF16), preferred_element_type=F32)


def _dot_nt(a, b):
    return lax.dot_general(a.astype(BF16), b.astype(BF16), (((1,), (1,)), ((), ())),
                           preferred_element_type=F32)


def _split3(x):
    hi = x.astype(BF16)
    rest = x - hi.astype(F32)
    mid = rest.astype(BF16)
    lo = (rest - mid.astype(F32)).astype(BF16)
    return lo, mid, hi


def _dot_f32_by_mask(x, mask01):
    mask01 = mask01.astype(BF16)
    lo, mid, hi = (jnp.dot(t, mask01, preferred_element_type=F32) for t in _split3(x))
    return lo + mid + hi


def _mask_dot_f32(mask01, x):
    mask01 = mask01.astype(BF16)
    lo, mid, hi = (jnp.dot(mask01, t, preferred_element_type=F32) for t in _split3(x))
    return lo + mid + hi


def _sigmoid(x):
    return 0.5 * jnp.tanh(0.5 * x) + 0.5


def _silu(x):
    return x * _sigmoid(x)


def _softplus(x):
    return jnp.maximum(x, 0.0) + jnp.log1p(jnp.exp(-jnp.abs(x)))


def _lower_tri(n):
    return lax.broadcasted_iota(jnp.int32, (n, n), 1) <= lax.broadcasted_iota(jnp.int32, (n, n), 0)


def _rmsnorm_rows(x, g):
    return x * lax.rsqrt(jnp.mean(x * x, axis=-1, keepdims=True) + EPS) * g


def _rmsnorm_body(x_ref, g_ref, o_ref):
    o_ref[...] = _rmsnorm_rows(x_ref[...], g_ref[...]).astype(o_ref.dtype)


def rmsnorm(x, g, out_dtype, tm=512):
    m, d = x.shape
    return pl.pallas_call(
        _rmsnorm_body,
        out_shape=jax.ShapeDtypeStruct((m, d), out_dtype),
        grid=(m // tm,),
        in_specs=[pl.BlockSpec((tm, d), lambda i: (i, 0)), pl.BlockSpec((1, d), lambda i: (0, 0))],
        out_specs=pl.BlockSpec((tm, d), lambda i: (i, 0)),
        compiler_params=_params("parallel"),
        name="rmsnorm",
    )(x, g.reshape(1, d))


def _residual_rows_copy(x_hbm, acc_ref, sem, tm):
    rows = pl.ds(pl.multiple_of(pl.program_id(0) * tm, tm), tm)
    return pltpu.make_async_copy(x_hbm.at[rows, :], acc_ref, sem)


def _ffn_body(h_ref, x_hbm, gn_ref, wg_ref, wu_ref, wo_ref, *refs, nf, tm, emit_x):
    xo_ref, ho_ref, sem = refs if emit_x else (refs[1], refs[0], refs[2])
    f = pl.program_id(1)

    @pl.when(f == 0)
    def _():
        _residual_rows_copy(x_hbm, xo_ref, sem, tm).start()

    tf = wg_ref.shape[1]
    gu = _dot(h_ref[...], jnp.concatenate([wg_ref[...].astype(BF16), wu_ref[...].astype(BF16)], axis=1))
    act = 0.5 * _silu(gu[:, :tf]) * gu[:, tf:]

    @pl.when(f == 0)
    def _():
        _residual_rows_copy(x_hbm, xo_ref, sem, tm).wait()

    xo_ref[...] += _dot(act, wo_ref[...])

    @pl.when(f == nf - 1)
    def _():
        ho_ref[...] = _rmsnorm_rows(xo_ref[...], gn_ref[...]).astype(ho_ref.dtype)


def ffn(h, x, w_in, w_out, layer, g_next, *, emit_x, h_dtype, tm=1024, tf=512):
    m, d = x.shape
    nf = D_FF // tf
    row_spec = pl.BlockSpec((tm, d), lambda i, f: (i, 0), pipeline_mode=pl.Buffered(1))
    out_shape = [jax.ShapeDtypeStruct((m, d), h_dtype)]
    out_specs = [row_spec]
    scratch = [pltpu.VMEM((tm, d), F32), pltpu.SemaphoreType.DMA(())]
    if emit_x:
        out_shape.insert(0, jax.ShapeDtypeStruct((m, d), F32))
        out_specs.insert(0, row_spec)
        scratch = scratch[1:]
    res = pl.pallas_call(
        functools.partial(_ffn_body, nf=nf, tm=tm, emit_x=emit_x),
        out_shape=out_shape,
        grid=(m // tm, nf),
        in_specs=[
            row_spec,
            pl.BlockSpec(memory_space=pl.ANY),
            pl.BlockSpec((1, d), lambda i, f: (0, 0)),
            pl.BlockSpec((None, d, tf), lambda i, f: (layer, 0, f)),
            pl.BlockSpec((None, d, tf), lambda i, f: (layer, 0, nf + f)),
            pl.BlockSpec((None, tf, d), lambda i, f: (layer, f, 0)),
        ],
        out_specs=out_specs,
        scratch_shapes=scratch,
        compiler_params=_params("parallel", "arbitrary"),
        name="ffn",
    )(h, x, g_next.reshape(1, d), w_in, w_in, w_out)
    return res if emit_x else res[0]


def _proj_body(h_ref, w_ref, o_ref, *, act):
    acc = _dot_nt(h_ref[...], w_ref[0])
    if act == "sigmoid":
        acc = _sigmoid(acc)
    o_ref[...] = acc.astype(o_ref.dtype)


def proj(h, w_t, layer, first_row, n, *, act=None, tm=2048, tn=512):
    m, k = h.shape
    assert first_row % 8 == 0 and n % tn == 0
    w_spec = pl.BlockSpec((pl.Element(1), pl.Element(tn), pl.Element(k)),
                          lambda i, j: (layer, pl.multiple_of(first_row + j * tn, 8), 0))
    return pl.pallas_call(
        functools.partial(_proj_body, act=act),
        out_shape=jax.ShapeDtypeStruct((m, n), BF16),
        grid=(m // tm, n // tn),
        in_specs=[pl.BlockSpec((tm, k), lambda i, j: (i, 0)), w_spec],
        out_specs=pl.BlockSpec((tm, tn), lambda i, j: (i, j)),
        compiler_params=_params("parallel", "arbitrary"),
        name="proj",
    )(h, w_t)


def _proj_small_body(h_ref, wf_ref, wd_ref, o_ref):
    k = h_ref.shape[1]
    pad = jnp.zeros((LANES - D_HEADS - B_HEADS, k), F32)
    w = jnp.concatenate([wf_ref[...], wd_ref[...], pad], axis=0)
    o_ref[...] = _dot_nt(h_ref[...], w)


def proj_small(h, w_t, layer, tm=2048):
    m, k = h.shape
    return pl.pallas_call(
        _proj_small_body,
        out_shape=jax.ShapeDtypeStruct((m, LANES), F32),
        grid=(m // tm,),
        in_specs=[
            pl.BlockSpec((tm, k), lambda i: (i, 0)),
            pl.BlockSpec((None, D_HEADS, k), lambda i: (layer, COL_FL // D_HEADS, 0)),
            pl.BlockSpec((None, B_HEADS, k), lambda i: (layer, COL_DT // B_HEADS, 0)),
        ],
        out_specs=pl.BlockSpec((tm, LANES), lambda i: (i, 0)),
        compiler_params=_params("parallel"),
        name="proj_small",
    )(h, w_t, w_t)


def _sgu_body(p_ref, lng_ref, w_ref, b_ref, o_ref):
    causal = _lower_tri(CHUNK)
    for ci in range(p_ref.shape[0] // CHUNK):
        rows = slice(ci * CHUNK, (ci + 1) * CHUNK)
        p = jax.nn.gelu(p_ref[rows, :].astype(F32))
        u = p[:, :WIDTH]
        v = p[:, WIDTH:]
        mu = jnp.mean(v, axis=-1, keepdims=True)
        var = jnp.mean(jnp.square(v - mu), axis=-1, keepdims=True)
        vn = (v - mu) * lax.rsqrt(var + EPS) * lng_ref[...]
        for g in range(A_GROUPS):
            cols = slice(g * LANES, (g + 1) * LANES)
            w = jnp.where(causal, w_ref[g], 0.0)
            mixed = _dot(w, vn[:, cols]) + b_ref[:, cols]
            o_ref[rows, cols] = (u[:, cols] * mixed).astype(o_ref.dtype)


def sgu(p_ab, ln_g, w_s, b_full, layer, rows=4 * CHUNK):
    m = p_ab.shape[0]
    return pl.pallas_call(
        _sgu_body,
        out_shape=jax.ShapeDtypeStruct((m, WIDTH), BF16),
        grid=(m // rows,),
        in_specs=[
            pl.BlockSpec((rows, 2 * WIDTH), lambda c: (c, 0)),
            pl.BlockSpec((1, WIDTH), lambda c: (0, 0)),
            pl.BlockSpec((None, A_GROUPS, CHUNK, CHUNK), lambda c: (layer, 0, 0, 0)),
            pl.BlockSpec((CHUNK, WIDTH), lambda c: (0, 0)),
        ],
        out_specs=pl.BlockSpec((rows, WIDTH), lambda c: (c, 0)),
        compiler_params=_params("parallel"),
        name="sgu",
    )(p_ab, ln_g.reshape(1, WIDTH), w_s, b_full)


def _ssd_body(z_ref, xbc_ref, small_ref, cw_ref, cb_ref, dtb_ref, alog_ref, dsk_ref, ng_ref, o_ref,
              xprev_ref, state_ref, y_ref):
    @pl.when(pl.program_id(0) == 0)
    def _():
        xprev_ref[...] = jnp.zeros_like(xprev_ref)
        state_ref[...] = jnp.zeros_like(state_ref)

    for b in range(z_ref.shape[0]):
        _ssd_chunk(z_ref.at[b], xbc_ref.at[b], small_ref.at[b], cw_ref, cb_ref, dtb_ref, alog_ref, dsk_ref, ng_ref,
                   o_ref.at[b], xprev_ref.at[b], state_ref.at[b], y_ref.at[b])


def _ssd_chunk(z_ref, xbc_ref, small_ref, cw_ref, cb_ref, dtb_ref, alog_ref, dsk_ref, ng_ref, o_ref,
               xprev_ref, state_ref, y_ref):
    xprev_ref[8:8 + CHUNK, :] = xbc_ref[...].astype(F32)
    conv = cb_ref[...]
    for j in range(B_CONV):
        conv = conv + cw_ref[j:j + 1, :] * xprev_ref[pl.ds(8 - (B_CONV - 1) + j, CHUNK), :]
    xprev_ref[0:8, :] = xprev_ref[CHUNK:CHUNK + 8, :]
    xbc = _silu(conv)
    xs = xbc[:, :WIDTH]
    bm = xbc[:, WIDTH:WIDTH + B_GROUPS * B_STATE]
    cm = xbc[:, WIDTH + B_GROUPS * B_STATE:]

    dt = _softplus(small_ref[...] + dtb_ref[...])
    a = -jnp.exp(alog_ref[...])
    causal = _lower_tri(CHUNK)
    acum = _mask_dot_f32(causal, dt * a)
    acum_t = acum.T
    expand = (lax.broadcasted_iota(jnp.int32, (LANES, WIDTH), 0) - SMALL_DT
              == lax.broadcasted_iota(jnp.int32, (LANES, WIDTH), 1) // B_HEAD_DIM)
    dt_e = _dot_f32_by_mask(dt, expand)
    acum_e = _dot_f32_by_mask(acum, expand)
    eacum_e = jnp.exp(acum_e)
    dend_e = jnp.exp(acum_e[CHUNK - 1:CHUNK, :] - acum_e)
    xdt = xs * dt_e
    xend = xdt * dend_e
    lane = lax.broadcasted_iota(jnp.int32, (CHUNK, LANES), 1)
    low_half = lane < B_HEAD_DIM
    hpg = B_HEADS // B_GROUPS
    gw = WIDTH // B_GROUPS

    for g in range(B_GROUPS):
        bg = bm[:, g * B_STATE:(g + 1) * B_STATE]
        cg = cm[:, g * B_STATE:(g + 1) * B_STATE]
        cb = _dot_nt(cg, bg)
        gcols = slice(g * gw, (g + 1) * gw)
        prev = state_ref[:, gcols]
        y_off = _dot(cg, prev) * eacum_e[:, gcols]
        for pair in range(hpg // 2):
            pcols = slice(g * gw + pair * LANES, g * gw + (pair + 1) * LANES)
            x_pair = xdt[:, pcols]
            y_pair = jnp.zeros((CHUNK, LANES), F32)
            for half in range(2):
                hl = SMALL_DT + g * hpg + 2 * pair + half
                seg = acum[:, hl:hl + 1] - acum_t[hl:hl + 1, :]
                decay = jnp.exp(jnp.where(causal, seg, -jnp.inf))
                x_half = jnp.where(low_half if half == 0 else ~low_half, x_pair, 0.0)
                y_pair = y_pair + _dot(cb * decay, x_half)
            y_pair = y_pair + y_off[:, pair * LANES:(pair + 1) * LANES] + xs[:, pcols] * dsk_ref[:, pcols]
            y_ref[:, pcols] = y_pair
        st = _dot(bg.T, xend[:, gcols])
        state_ref[:, gcols] = prev * eacum_e[CHUNK - 1:CHUNK, gcols] + st

    y = y_ref[...] * _silu(z_ref[...].astype(F32))
    for g in range(B_GROUPS):
        gcols = slice(g * gw, (g + 1) * gw)
        yg = y[:, gcols]
        yg = yg * lax.rsqrt(jnp.mean(yg * yg, axis=-1, keepdims=True) + EPS)
        o_ref[:, gcols] = (yg * ng_ref[:, gcols]).astype(o_ref.dtype)


def ssd(p_ab, p_small, conv_w, conv_b, dtb_row, alog_row, dskip_row, norm_g, layer, bsz, seq):
    m = bsz * seq
    p3 = p_ab.reshape(bsz, seq, p_ab.shape[1])
    const2 = lambda c: (0, 0)
    y = pl.pallas_call(
        _ssd_body,
        out_shape=jax.ShapeDtypeStruct((bsz, seq, WIDTH), BF16),
        grid=(seq // CHUNK,),
        in_specs=[
            pl.BlockSpec((bsz, CHUNK, WIDTH), lambda c: (0, c, 2)),
            pl.BlockSpec((bsz, CHUNK, B_CONV_DIM), lambda c: (0, c, 2)),
            pl.BlockSpec((bsz, CHUNK, LANES), lambda c: (0, c, 0)),
            pl.BlockSpec((None, B_CONV, B_CONV_DIM), lambda c: (layer, 0, 0)),
            pl.BlockSpec((None, 1, B_CONV_DIM), lambda c: (layer, 0, 0)),
            pl.BlockSpec((1, LANES), const2),
            pl.BlockSpec((1, LANES), const2),
            pl.BlockSpec((1, WIDTH), const2),
            pl.BlockSpec((1, WIDTH), const2),
        ],
        out_specs=pl.BlockSpec((bsz, CHUNK, WIDTH), lambda c: (0, c, 0)),
        scratch_shapes=[pltpu.VMEM((bsz, CHUNK + 8, B_CONV_DIM), F32), pltpu.VMEM((bsz, B_STATE, WIDTH), F32),
                        pltpu.VMEM((bsz, CHUNK, WIDTH), F32)],
        compiler_params=_params("arbitrary"),
        name="ssd",
    )(p3, p3, p_small.reshape(bsz, seq, LANES), conv_w, conv_b.reshape(conv_b.shape[0], 1, B_CONV_DIM), dtb_row,
      alog_row, dskip_row, norm_g.reshape(1, WIDTH))
    return y.reshape(m, WIDTH)


def _retention_tables(seq):
    half = C_HEAD_QK // 2
    inv_freq = 1.0 / (ROPE_BASE ** (jnp.arange(half, dtype=F32) / half))
    ang = jnp.arange(seq, dtype=F32)[:, None] * inv_freq[None, :]
    cos = jnp.tile(jnp.cos(ang), (1, 2 * LANES // C_HEAD_QK))
    sin = jnp.tile(jnp.concatenate([-jnp.sin(ang), jnp.sin(ang)], axis=1), (1, LANES // C_HEAD_QK))
    log_gamma = jnp.log(1.0 - 2.0 ** (-5.0 - jnp.arange(C_HEADS, dtype=F32)))
    idx = jnp.arange(CHUNK, dtype=F32)
    mask = idx[:, None] >= idx[None, :]
    rel = jnp.where(mask, idx[:, None] - idx[None, :], 0.0)
    intra = (jnp.exp(rel[..., None] * log_gamma) * mask[..., None]).transpose(2, 0, 1)
    k_decay = jnp.repeat(jnp.exp((CHUNK - 1.0 - idx)[:, None] * log_gamma), C_HEAD_QK, axis=1)
    q_decay = jnp.repeat(jnp.exp((idx + 1.0)[:, None] * log_gamma), C_HEAD_QK, axis=1)
    chunk_decay = jnp.broadcast_to(jnp.repeat(jnp.exp(CHUNK * log_gamma), C_HEAD_QK)[:, None],
                                   (C_QK, LANES))
    return cos, sin, intra, k_decay, q_decay, chunk_decay


def _retention_body(qk_ref, v_ref, g_ref, cos_ref, sin_ref, intra_ref, kdec_ref, qdec_ref, cdec_ref,
                    small_ref, fb_ref, o_ref, cum_ref, cum_t_ref, state_ref, carry_ref):
    @pl.when(pl.program_id(0) == 0)
    def _():
        state_ref[...] = jnp.zeros_like(state_ref)
        carry_ref[...] = jnp.zeros_like(carry_ref)

    for b in range(qk_ref.shape[0]):
        _retention_chunk(qk_ref.at[b], v_ref.at[b], g_ref.at[b], cos_ref, sin_ref, intra_ref, kdec_ref, qdec_ref,
                         cdec_ref, small_ref.at[b], fb_ref, o_ref.at[b], cum_ref.at[b], cum_t_ref.at[b],
                         state_ref.at[b], carry_ref.at[b])


def _retention_chunk(qk_ref, v_ref, g_ref, cos_ref, sin_ref, intra_ref, kdec_ref, qdec_ref, cdec_ref,
                     small_ref, fb_ref, o_ref, cum_ref, cum_t_ref, state_ref, carry_ref):
    log_f = -_softplus(-(small_ref[...] + fb_ref[...]))
    cum = _mask_dot_f32(_lower_tri(CHUNK), log_f) + carry_ref[...]
    carry_ref[...] = cum[CHUNK - 1:CHUNK, :]
    cum_ref[...] = cum
    cum_t_ref[...] = cum.T

    lane = lax.broadcasted_iota(jnp.int32, (CHUNK, LANES), 1)
    first_half = (lane % C_HEAD_QK) < (C_HEAD_QK // 2)
    low_head = lane < C_HEAD_QK
    low_rows = lax.broadcasted_iota(jnp.int32, (LANES, LANES), 0) < C_HEAD_QK
    cos = cos_ref[...]
    sin = sin_ref[...]

    def rotary(x):
        partner = jnp.where(first_half, pltpu.roll(x, LANES - C_HEAD_QK // 2, 1), pltpu.roll(x, C_HEAD_QK // 2, 1))
        return x * cos + partner * sin

    for pair in range(C_HEADS // 2):
        pc = slice(pair * LANES, (pair + 1) * LANES)
        q2 = rotary(qk_ref[:, pc].astype(F32))
        k2 = rotary(qk_ref[:, C_QK + pair * LANES:C_QK + (pair + 1) * LANES].astype(F32)) * (C_HEAD_QK ** -0.5)
        qd2 = q2 * qdec_ref[:, pc]
        kd2_t = (k2 * kdec_ref[:, pc]).T
        prev = state_ref[pc, :]
        new_state = []
        for half in range(2):
            h = 2 * pair + half
            hc = slice(h * LANES, (h + 1) * LANES)
            sel = low_head if half == 0 else ~low_head
            scores = _dot_nt(jnp.where(sel, q2, 0.0), k2) * intra_ref[h]
            vh = v_ref[:, hc]
            y = _dot(scores, vh) + _dot(jnp.where(sel, qd2, 0.0), prev)
            mu = jnp.mean(y, axis=-1, keepdims=True)
            var = jnp.mean(jnp.square(y - mu), axis=-1, keepdims=True)
            y = (y - mu) * lax.rsqrt(var + EPS)
            o_ref[:, hc] = (_silu(g_ref[:, hc].astype(F32)) * y).astype(o_ref.dtype)
            new_state.append(_dot(kd2_t, vh))
        state_ref[pc, :] = prev * cdec_ref[pc, :] + jnp.where(low_rows, new_state[0], new_state[1])


def retention(p_cd, p_small, fb_row, tables, bsz, seq):
    m = bsz * seq
    cos, sin, intra, k_decay, q_decay, chunk_decay = tables
    const2 = lambda c: (0, 0)
    p3 = p_cd.reshape(bsz, seq, p_cd.shape[1])
    y, cum, cum_t = pl.pallas_call(
        _retention_body,
        out_shape=[jax.ShapeDtypeStruct((bsz, seq, WIDTH), BF16), jax.ShapeDtypeStruct((bsz, seq, LANES), F32),
                   jax.ShapeDtypeStruct((bsz, LANES, seq), F32)],
        grid=(seq // CHUNK,),
        in_specs=[
            pl.BlockSpec((bsz, CHUNK, 2 * C_QK), lambda c: (0, c, 0)),
            pl.BlockSpec((bsz, CHUNK, WIDTH), lambda c: (0, c, 1)),
            pl.BlockSpec((bsz, CHUNK, WIDTH), lambda c: (0, c, 2)),
            pl.BlockSpec((CHUNK, LANES), lambda c: (c, 0)),
            pl.BlockSpec((CHUNK, LANES), lambda c: (c, 0)),
            pl.BlockSpec((C_HEADS, CHUNK, CHUNK), lambda c: (0, 0, 0)),
            pl.BlockSpec((CHUNK, C_QK), const2),
            pl.BlockSpec((CHUNK, C_QK), const2),
            pl.BlockSpec((C_QK, LANES), const2),
            pl.BlockSpec((bsz, CHUNK, LANES), lambda c: (0, c, 0)),
            pl.BlockSpec((1, LANES), const2),
        ],
        out_specs=[pl.BlockSpec((bsz, CHUNK, WIDTH), lambda c: (0, c, 0)),
                   pl.BlockSpec((bsz, CHUNK, LANES), lambda c: (0, c, 0)),
                   pl.BlockSpec((bsz, LANES, CHUNK), lambda c: (0, 0, c))],
        scratch_shapes=[pltpu.VMEM((bsz, C_QK, LANES), F32), pltpu.VMEM((bsz, 1, LANES), F32)],
        compiler_params=_params("arbitrary"),
        name="retention",
    )(p3, p3, p3, cos, sin, intra, k_decay, q_decay, chunk_decay, p_small.reshape(bsz, seq, LANES), fb_row)
    return y.reshape(m, WIDTH), cum.reshape(m, LANES), cum_t


def _fox_body(q_ref, k_ref, v_ref, cum_ref, cum_t_ref, o_ref, *, blk, hps):
    hg = pl.program_id(1)
    qi = pl.program_id(2)
    lane = lax.broadcasted_iota(jnp.int32, (blk, LANES), 1)
    scale = LOG2E * D_HEAD_DIM ** -0.5
    cols = [slice(i * D_HEAD_DIM, (i + 1) * D_HEAD_DIM) for i in range(hps)]
    q = [q_ref[:, c].astype(BF16) for c in cols]
    cq = [LOG2E * jnp.sum(jnp.where(lane == SMALL_FL + hg * hps + i, cum_ref[...], 0.0), axis=-1, keepdims=True)
          for i in range(hps)]

    def scores(i, kj):
        ks = pl.multiple_of(kj * blk, blk)
        s = _dot_nt(q[i], k_ref[pl.ds(ks, blk), cols[i]]) * scale
        return s + cq[i] - LOG2E * cum_t_ref[pl.ds(SMALL_FL + hg * hps + i, 1), pl.ds(ks, blk)], ks

    def update(i, carry, s, ks):
        m_prev, l_prev, acc = carry
        m_new = jnp.maximum(m_prev, jnp.max(s, axis=-1, keepdims=True))
        alpha = jnp.exp2(m_prev - m_new)
        p = jnp.exp2(s - m_new)
        l_new = alpha * l_prev + jnp.sum(p, axis=-1, keepdims=True)
        acc = alpha * acc + _dot(p, v_ref[pl.ds(ks, blk), cols[i]])
        return m_new, l_new, acc

    def full_block(kj, carries):
        return tuple(update(i, carries[i], *scores(i, kj)) for i in range(hps))

    init = (jnp.full((blk, 1), -jnp.inf, F32), jnp.zeros((blk, 1), F32), jnp.zeros((blk, D_HEAD_DIM), F32))
    carries = lax.fori_loop(0, qi, full_block, (init,) * hps)
    causal = _lower_tri(blk)
    for i in range(hps):
        s, ks = scores(i, qi)
        _, l_fin, acc = update(i, carries[i], jnp.where(causal, s, -jnp.inf), ks)
        o_ref[:, cols[i]] = (acc / l_fin).astype(o_ref.dtype)


def fox(p_cd, cum, cum_t, bsz, seq, blk=512, hps=4):
    nq = seq // blk
    width = hps * D_HEAD_DIM
    col0 = (2 * C_QK + 2 * WIDTH) // width
    ng = D_HEADS // hps
    return pl.pallas_call(
        functools.partial(_fox_body, blk=blk, hps=hps),
        out_shape=jax.ShapeDtypeStruct((bsz * seq, WIDTH), BF16),
        grid=(bsz, ng, nq),
        in_specs=[
            pl.BlockSpec((blk, width), lambda b, g, i: (b * nq + i, col0 + g)),
            pl.BlockSpec((seq, width), lambda b, g, i: (b, col0 + ng + g)),
            pl.BlockSpec((seq, width), lambda b, g, i: (b, col0 + 2 * ng + g)),
            pl.BlockSpec((blk, LANES), lambda b, g, i: (b * nq + i, 0)),
            pl.BlockSpec((None, 8, seq), lambda b, g, i: (b, SMALL_FL // 8, 0)),
        ],
        out_specs=pl.BlockSpec((blk, width), lambda b, g, i: (b * nq + i, g)),
        compiler_params=_params("parallel", "parallel", "arbitrary"),
        name="fox",
    )(p_cd, p_cd, p_cd, cum, cum_t)


def _merge_body(ya_ref, yb_ref, yc_ref, yd_ref, ga_ref, gb_ref, gc_ref, gd_ref, wb_ref, o_ref):
    merged = ga_ref[...].astype(F32) * _dot(ya_ref[...], wb_ref[0])
    merged = merged + gb_ref[...].astype(F32) * _dot(yb_ref[...], wb_ref[1])
    merged = merged + gc_ref[...].astype(F32) * _dot(yc_ref[...], wb_ref[2])
    merged = merged + gd_ref[...].astype(F32) * _dot(yd_ref[...], wb_ref[3])
    o_ref[...] = merged.astype(o_ref.dtype)


def merge(ys, gates, w_branch, layer, tm=1024, tn=512):
    m = ys[0].shape[0]
    d = w_branch.shape[-1]
    nj = d // tn
    y_spec = pl.BlockSpec((tm, WIDTH), lambda i, j: (i, 0))
    gate_specs = [pl.BlockSpec((tm, tn), functools.partial(lambda i, j, n: (i, n * nj + j), n=n)) for n in range(4)]
    return pl.pallas_call(
        _merge_body,
        out_shape=jax.ShapeDtypeStruct((m, d), BF16),
        grid=(m // tm, nj),
        in_specs=[y_spec] * 4 + gate_specs + [pl.BlockSpec((None, 4, WIDTH, tn), lambda i, j: (layer, 0, 0, j))],
        out_specs=pl.BlockSpec((tm, tn), lambda i, j: (i, j)),
        compiler_params=_params("parallel", "arbitrary"),
        name="merge",
    )(*ys, gates, gates, gates, gates, w_branch)


def _out_proj_body(a_ref, x_ref, gn_ref, w_ref, xo_ref, ho_ref):
    x_new = x_ref[...] + _dot(a_ref[...], w_ref[...])
    xo_ref[...] = x_new
    ho_ref[...] = _rmsnorm_rows(x_new, gn_ref[...]).astype(ho_ref.dtype)


def out_proj(a, x, w, layer, g_next, tm=512):
    m, d = x.shape
    k = a.shape[1]
    row_spec = pl.BlockSpec((tm, d), lambda i: (i, 0))
    return pl.pallas_call(
        _out_proj_body,
        out_shape=[jax.ShapeDtypeStruct((m, d), F32), jax.ShapeDtypeStruct((m, d), BF16)],
        grid=(m // tm,),
        in_specs=[
            pl.BlockSpec((tm, k), lambda i: (i, 0)),
            row_spec,
            pl.BlockSpec((1, d), lambda i: (0, 0)),
            pl.BlockSpec((None, k, d), lambda i: (layer, 0, 0), pipeline_mode=pl.Buffered(1)),
        ],
        out_specs=[row_spec, row_spec],
        compiler_params=_params("parallel"),
        name="out_proj",
    )(a, x, g_next.reshape(1, d), w)


def _pad_lanes(v, offset):
    return jnp.zeros((1, LANES), F32).at[0, offset:offset + v.shape[0]].set(v.astype(F32))


def kernel(x, ffn1_norm, ffn1_w_in, ffn1_w_out, mix_norm, w_mix_in, sgu_norm, sgu_w, sgu_b, conv_w, conv_b,
           dt_bias, a_log, d_skip, ssm_norm, forget_bias, w_branch, w_mix_out, ffn2_norm, ffn2_w_in, ffn2_w_out,
           final_norm):
    bsz, seq, d = x.shape
    depth = ffn1_norm.shape[0]
    m = bsz * seq
    x = x.reshape(m, d)
    tables = _retention_tables(seq)
    w_t = jnp.swapaxes(w_mix_in, 1, 2)
    h = rmsnorm(x, ffn1_norm[0], BF16)
    for l in range(depth):
        x, h = ffn(h, x, ffn1_w_in, ffn1_w_out, l, mix_norm[l], emit_x=True, h_dtype=BF16)

        p_ab = proj(h, w_t, l, 0, N_AB)
        p_small = proj_small(h, w_t, l)
        p_cd = proj(h, w_t, l, COL_C, N_CD)
        gates = proj(h, w_t, l, COL_G, 4 * d, act="sigmoid")

        b_full = jnp.repeat(sgu_b[l].T, LANES, axis=1)
        y_a = sgu(p_ab, sgu_norm[l], sgu_w, b_full, l)
        y_b = ssd(p_ab, p_small, conv_w, conv_b, _pad_lanes(dt_bias[l], SMALL_DT), _pad_lanes(a_log[l], SMALL_DT),
                  jnp.repeat(d_skip[l], B_HEAD_DIM).reshape(1, WIDTH), ssm_norm[l], l, bsz, seq)
        y_c, cum, cum_t = retention(p_cd, p_small, _pad_lanes(forget_bias[l], SMALL_FL), tables, bsz, seq)
        y_d = fox(p_cd, cum, cum_t, bsz, seq)

        merged = merge((y_a, y_b, y_c, y_d), gates, w_branch, l)
        x, h = out_proj(merged, x, w_mix_out, l, ffn2_norm[l])
        if l + 1 < depth:
            x, h = ffn(h, x, ffn2_w_in, ffn2_w_out, l, ffn1_norm[l + 1], emit_x=True, h_dtype=BF16)
        else:
            h = ffn(h, x, ffn2_w_in, ffn2_w_out, l, final_norm, emit_x=False, h_dtype=F32)
    return h.reshape(bsz, seq, d)
```

```python
import functools

import jax
import jax.numpy as jnp
from jax import lax
from jax.experimental import pallas as pl
from jax.experimental.pallas import tpu as pltpu

F32 = jnp.float32
BF16 = jnp.bfloat16

EPS = 1e-6
LOG2E = 1.4426950408889634
CHUNK = 128
LANES = 128
VMEM_LIMIT = 56 * 1024 * 1024

D_MODEL = 2048
D_FF = 5632
WIDTH = 1024
A_GROUPS = 8
B_HEADS = 16
B_HEAD_DIM = 64
B_STATE = 128
B_GROUPS = 2
B_CONV = 4
B_CONV_DIM = WIDTH + 2 * B_GROUPS * B_STATE
C_HEADS = 8
C_HEAD_QK = 64
C_QK = C_HEADS * C_HEAD_QK
D_HEADS = 8
D_HEAD_DIM = 128
ROPE_BASE = 10000.0

COL_DT = 2 * WIDTH + WIDTH + B_CONV_DIM
COL_C = COL_DT + B_HEADS
COL_FL = COL_C + 2 * C_QK + 2 * WIDTH + 3 * WIDTH
COL_G = COL_FL + D_HEADS
N_AB = COL_DT
N_CD = COL_FL - COL_C
SMALL_FL = 0
SMALL_DT = D_HEADS


def _params(*sem):
    return pltpu.CompilerParams(dimension_semantics=sem, vmem_limit_bytes=VMEM_LIMIT)


def _dot(a, b):
    return jnp.dot(a.astype(BF16), b.astype(BF16), preferred_element_type=F32)


def _dot_nt(a, b):
    return lax.dot_general(a.astype(BF16), b.astype(BF16), (((1,), (1,)), ((), ())),
                           preferred_element_type=F32)


def _split3(x):
    hi = x.astype(BF16)
    rest = x - hi.astype(F32)
    mid = rest.astype(BF16)
    lo = (rest - mid.astype(F32)).astype(BF16)
    return lo, mid, hi


def _dot_f32_by_mask(x, mask01):
    mask01 = mask01.astype(BF16)
    lo, mid, hi = (jnp.dot(t, mask01, preferred_element_type=F32) for t in _split3(x))
    return lo + mid + hi


def _mask_dot_f32(mask01, x):
    mask01 = mask01.astype(BF16)
    lo, mid, hi = (jnp.dot(mask01, t, preferred_element_type=F32) for t in _split3(x))
    return lo + mid + hi


def _sigmoid(x):
    return 0.5 * jnp.tanh(0.5 * x) + 0.5


def _silu(x):
    return x * _sigmoid(x)


def _softplus(x):
    return jnp.maximum(x, 0.0) + jnp.log1p(jnp.exp(-jnp.abs(x)))


def _lower_tri(n):
    return lax.broadcasted_iota(jnp.int32, (n, n), 1) <= lax.broadcasted_iota(jnp.int32, (n, n), 0)


def _rmsnorm_rows(x, g):
    return x * lax.rsqrt(jnp.mean(x * x, axis=-1, keepdims=True) + EPS) * g


def _rmsnorm_body(x_ref, g_ref, o_ref):
    o_ref[...] = _rmsnorm_rows(x_ref[...], g_ref[...]).astype(o_ref.dtype)


def rmsnorm(x, g, out_dtype, tm=512):
    m, d = x.shape
    return pl.pallas_call(
        _rmsnorm_body,
        out_shape=jax.ShapeDtypeStruct((m, d), out_dtype),
        grid=(m // tm,),
        in_specs=[pl.BlockSpec((tm, d), lambda i: (i, 0)), pl.BlockSpec((1, d), lambda i: (0, 0))],
        out_specs=pl.BlockSpec((tm, d), lambda i: (i, 0)),
        compiler_params=_params("parallel"),
        name="rmsnorm",
    )(x, g.reshape(1, d))


def _residual_rows_copy(x_hbm, acc_ref, sem, tm):
    rows = pl.ds(pl.multiple_of(pl.program_id(0) * tm, tm), tm)
    return pltpu.make_async_copy(x_hbm.at[rows, :], acc_ref, sem)


def _ffn_body(h_ref, x_hbm, gn_ref, wg_ref, wu_ref, wo_ref, *refs, nf, tm, emit_x):
    xo_ref, ho_ref, sem = refs if emit_x else (refs[1], refs[0], refs[2])
    f = pl.program_id(1)

    @pl.when(f == 0)
    def _():
        _residual_rows_copy(x_hbm, xo_ref, sem, tm).start()

    tf = wg_ref.shape[1]
    gu = _dot(h_ref[...], jnp.concatenate([wg_ref[...].astype(BF16), wu_ref[...].astype(BF16)], axis=1))
    act = 0.5 * _silu(gu[:, :tf]) * gu[:, tf:]

    @pl.when(f == 0)
    def _():
        _residual_rows_copy(x_hbm, xo_ref, sem, tm).wait()

    xo_ref[...] += _dot(act, wo_ref[...])

    @pl.when(f == nf - 1)
    def _():
        ho_ref[...] = _rmsnorm_rows(xo_ref[...], gn_ref[...]).astype(ho_ref.dtype)


def ffn(h, x, w_in, w_out, layer, g_next, *, emit_x, h_dtype, tm=1024, tf=512):
    m, d = x.shape
    nf = D_FF // tf
    row_spec = pl.BlockSpec((tm, d), lambda i, f: (i, 0), pipeline_mode=pl.Buffered(1))
    out_shape = [jax.ShapeDtypeStruct((m, d), h_dtype)]
    out_specs = [row_spec]
    scratch = [pltpu.VMEM((tm, d), F32), pltpu.SemaphoreType.DMA(())]
    if emit_x:
        out_shape.insert(0, jax.ShapeDtypeStruct((m, d), F32))
        out_specs.insert(0, row_spec)
        scratch = scratch[1:]
    res = pl.pallas_call(
        functools.partial(_ffn_body, nf=nf, tm=tm, emit_x=emit_x),
        out_shape=out_shape,
        grid=(m // tm, nf),
        in_specs=[
            row_spec,
            pl.BlockSpec(memory_space=pl.ANY),
            pl.BlockSpec((1, d), lambda i, f: (0, 0)),
            pl.BlockSpec((None, d, tf), lambda i, f: (layer, 0, f)),
            pl.BlockSpec((None, d, tf), lambda i, f: (layer, 0, nf + f)),
            pl.BlockSpec((None, tf, d), lambda i, f: (layer, f, 0)),
        ],
        out_specs=out_specs,
        scratch_shapes=scratch,
        compiler_params=_params("parallel", "arbitrary"),
        name="ffn",
    )(h, x, g_next.reshape(1, d), w_in, w_in, w_out)
    return res if emit_x else res[0]


def _proj_body(h_ref, w_ref, o_ref, *, act):
    acc = _dot_nt(h_ref[...], w_ref[0])
    if act == "sigmoid":
        acc = _sigmoid(acc)
    o_ref[...] = acc.astype(o_ref.dtype)


def proj(h, w_t, layer, first_row, n, *, act=None, tm=2048, tn=512):
    m, k = h.shape
    assert first_row % 8 == 0 and n % tn == 0
    w_spec = pl.BlockSpec((pl.Element(1), pl.Element(tn), pl.Element(k)),
                          lambda i, j: (layer, pl.multiple_of(first_row + j * tn, 8), 0))
    return pl.pallas_call(
        functools.partial(_proj_body, act=act),
        out_shape=jax.ShapeDtypeStruct((m, n), BF16),
        grid=(m // tm, n // tn),
        in_specs=[pl.BlockSpec((tm, k), lambda i, j: (i, 0)), w_spec],
        out_specs=pl.BlockSpec((tm, tn), lambda i, j: (i, j)),
        compiler_params=_params("parallel", "arbitrary"),
        name="proj",
    )(h, w_t)


def _proj_small_body(h_ref, wf_ref, wd_ref, o_ref):
    k = h_ref.shape[1]
    pad = jnp.zeros((LANES - D_HEADS - B_HEADS, k), F32)
    w = jnp.concatenate([wf_ref[...], wd_ref[...], pad], axis=0)
    o_ref[...] = _dot_nt(h_ref[...], w)


def proj_small(h, w_t, layer, tm=2048):
    m, k = h.shape
    return pl.pallas_call(
        _proj_small_body,
        out_shape=jax.ShapeDtypeStruct((m, LANES), F32),
        grid=(m // tm,),
        in_specs=[
            pl.BlockSpec((tm, k), lambda i: (i, 0)),
            pl.BlockSpec((None, D_HEADS, k), lambda i: (layer, COL_FL // D_HEADS, 0)),
            pl.BlockSpec((None, B_HEADS, k), lambda i: (layer, COL_DT // B_HEADS, 0)),
        ],
        out_specs=pl.BlockSpec((tm, LANES), lambda i: (i, 0)),
        compiler_params=_params("parallel"),
        name="proj_small",
    )(h, w_t, w_t)


def _sgu_body(p_ref, lng_ref, w_ref, b_ref, o_ref):
    causal = _lower_tri(CHUNK)
    for ci in range(p_ref.shape[0] // CHUNK):
        rows = slice(ci * CHUNK, (ci + 1) * CHUNK)
        p = jax.nn.gelu(p_ref[rows, :].astype(F32))
        u = p[:, :WIDTH]
        v = p[:, WIDTH:]
        mu = jnp.mean(v, axis=-1, keepdims=True)
        var = jnp.mean(jnp.square(v - mu), axis=-1, keepdims=True)
        vn = (v - mu) * lax.rsqrt(var + EPS) * lng_ref[...]
        for g in range(A_GROUPS):
            cols = slice(g * LANES, (g + 1) * LANES)
            w = jnp.where(causal, w_ref[g], 0.0)
            mixed = _dot(w, vn[:, cols]) + b_ref[:, cols]
            o_ref[rows, cols] = (u[:, cols] * mixed).astype(o_ref.dtype)


def sgu(p_ab, ln_g, w_s, b_full, layer, rows=4 * CHUNK):
    m = p_ab.shape[0]
    return pl.pallas_call(
        _sgu_body,
        out_shape=jax.ShapeDtypeStruct((m, WIDTH), BF16),
        grid=(m // rows,),
        in_specs=[
            pl.BlockSpec((rows, 2 * WIDTH), lambda c: (c, 0)),
            pl.BlockSpec((1, WIDTH), lambda c: (0, 0)),
            pl.BlockSpec((None, A_GROUPS, CHUNK, CHUNK), lambda c: (layer, 0, 0, 0)),
            pl.BlockSpec((CHUNK, WIDTH), lambda c: (0, 0)),
        ],
        out_specs=pl.BlockSpec((rows, WIDTH), lambda c: (c, 0)),
        compiler_params=_params("parallel"),
        name="sgu",
    )(p_ab, ln_g.reshape(1, WIDTH), w_s, b_full)


def _ssd_body(z_ref, xbc_ref, small_ref, cw_ref, cb_ref, dtb_ref, alog_ref, dsk_ref, ng_ref, o_ref,
              xprev_ref, state_ref, y_ref):
    @pl.when(pl.program_id(0) == 0)
    def _():
        xprev_ref[...] = jnp.zeros_like(xprev_ref)
        state_ref[...] = jnp.zeros_like(state_ref)

    for b in range(z_ref.shape[0]):
        _ssd_chunk(z_ref.at[b], xbc_ref.at[b], small_ref.at[b], cw_ref, cb_ref, dtb_ref, alog_ref, dsk_ref, ng_ref,
                   o_ref.at[b], xprev_ref.at[b], state_ref.at[b], y_ref.at[b])


def _ssd_chunk(z_ref, xbc_ref, small_ref, cw_ref, cb_ref, dtb_ref, alog_ref, dsk_ref, ng_ref, o_ref,
               xprev_ref, state_ref, y_ref):
    xprev_ref[8:8 + CHUNK, :] = xbc_ref[...].astype(F32)
    conv = cb_ref[...]
    for j in range(B_CONV):
        conv = conv + cw_ref[j:j + 1, :] * xprev_ref[pl.ds(8 - (B_CONV - 1) + j, CHUNK), :]
    xprev_ref[0:8, :] = xprev_ref[CHUNK:CHUNK + 8, :]
    xbc = _silu(conv)
    xs = xbc[:, :WIDTH]
    bm = xbc[:, WIDTH:WIDTH + B_GROUPS * B_STATE]
    cm = xbc[:, WIDTH + B_GROUPS * B_STATE:]

    dt = _softplus(small_ref[...] + dtb_ref[...])
    a = -jnp.exp(alog_ref[...])
    causal = _lower_tri(CHUNK)
    acum = _mask_dot_f32(causal, dt * a)
    acum_t = acum.T
    expand = (lax.broadcasted_iota(jnp.int32, (LANES, WIDTH), 0) - SMALL_DT
              == lax.broadcasted_iota(jnp.int32, (LANES, WIDTH), 1) // B_HEAD_DIM)
    dt_e = _dot_f32_by_mask(dt, expand)
    acum_e = _dot_f32_by_mask(acum, expand)
    eacum_e = jnp.exp(acum_e)
    dend_e = jnp.exp(acum_e[CHUNK - 1:CHUNK, :] - acum_e)
    xdt = xs * dt_e
    xend = xdt * dend_e
    lane = lax.broadcasted_iota(jnp.int32, (CHUNK, LANES), 1)
    low_half = lane < B_HEAD_DIM
    hpg = B_HEADS // B_GROUPS
    gw = WIDTH // B_GROUPS

    for g in range(B_GROUPS):
        bg = bm[:, g * B_STATE:(g + 1) * B_STATE]
        cg = cm[:, g * B_STATE:(g + 1) * B_STATE]
        cb = _dot_nt(cg, bg)
        gcols = slice(g * gw, (g + 1) * gw)
        prev = state_ref[:, gcols]
        y_off = _dot(cg, prev) * eacum_e[:, gcols]
        for pair in range(hpg // 2):
            pcols = slice(g * gw + pair * LANES, g * gw + (pair + 1) * LANES)
            x_pair = xdt[:, pcols]
            y_pair = jnp.zeros((CHUNK, LANES), F32)
            for half in range(2):
                hl = SMALL_DT + g * hpg + 2 * pair + half
                seg = acum[:, hl:hl + 1] - acum_t[hl:hl + 1, :]
                decay = jnp.exp(jnp.where(causal, seg, -jnp.inf))
                x_half = jnp.where(low_half if half == 0 else ~low_half, x_pair, 0.0)
                y_pair = y_pair + _dot(cb * decay, x_half)
            y_pair = y_pair + y_off[:, pair * LANES:(pair + 1) * LANES] + xs[:, pcols] * dsk_ref[:, pcols]
            y_ref[:, pcols] = y_pair
        st = _dot(bg.T, xend[:, gcols])
        state_ref[:, gcols] = prev * eacum_e[CHUNK - 1:CHUNK, gcols] + st

    y = y_ref[...] * _silu(z_ref[...].astype(F32))
    for g in range(B_GROUPS):
        gcols = slice(g * gw, (g + 1) * gw)
        yg = y[:, gcols]
        yg = yg * lax.rsqrt(jnp.mean(yg * yg, axis=-1, keepdims=True) + EPS)
        o_ref[:, gcols] = (yg * ng_ref[:, gcols]).astype(o_ref.dtype)


def ssd(p_ab, p_small, conv_w, conv_b, dtb_row, alog_row, dskip_row, norm_g, layer, bsz, seq):
    m = bsz * seq
    p3 = p_ab.reshape(bsz, seq, p_ab.shape[1])
    const2 = lambda c: (0, 0)
    y = pl.pallas_call(
        _ssd_body,
        out_shape=jax.ShapeDtypeStruct((bsz, seq, WIDTH), BF16),
        grid=(seq // CHUNK,),
        in_specs=[
            pl.BlockSpec((bsz, CHUNK, WIDTH), lambda c: (0, c, 2)),
            pl.BlockSpec((bsz, CHUNK, B_CONV_DIM), lambda c: (0, c, 2)),
            pl.BlockSpec((bsz, CHUNK, LANES), lambda c: (0, c, 0)),
            pl.BlockSpec((None, B_CONV, B_CONV_DIM), lambda c: (layer, 0, 0)),
            pl.BlockSpec((None, 1, B_CONV_DIM), lambda c: (layer, 0, 0)),
            pl.BlockSpec((1, LANES), const2),
            pl.BlockSpec((1, LANES), const2),
            pl.BlockSpec((1, WIDTH), const2),
            pl.BlockSpec((1, WIDTH), const2),
        ],
        out_specs=pl.BlockSpec((bsz, CHUNK, WIDTH), lambda c: (0, c, 0)),
        scratch_shapes=[pltpu.VMEM((bsz, CHUNK + 8, B_CONV_DIM), F32), pltpu.VMEM((bsz, B_STATE, WIDTH), F32),
                        pltpu.VMEM((bsz, CHUNK, WIDTH), F32)],
        compiler_params=_params("arbitrary"),
        name="ssd",
    )(p3, p3, p_small.reshape(bsz, seq, LANES), conv_w, conv_b.reshape(conv_b.shape[0], 1, B_CONV_DIM), dtb_row,
      alog_row, dskip_row, norm_g.reshape(1, WIDTH))
    return y.reshape(m, WIDTH)


def _retention_tables(seq):
    half = C_HEAD_QK // 2
    inv_freq = 1.0 / (ROPE_BASE ** (jnp.arange(half, dtype=F32) / half))
    ang = jnp.arange(seq, dtype=F32)[:, None] * inv_freq[None, :]
    cos = jnp.tile(jnp.cos(ang), (1, 2 * LANES // C_HEAD_QK))
    sin = jnp.tile(jnp.concatenate([-jnp.sin(ang), jnp.sin(ang)], axis=1), (1, LANES // C_HEAD_QK))
    log_gamma = jnp.log(1.0 - 2.0 ** (-5.0 - jnp.arange(C_HEADS, dtype=F32)))
    idx = jnp.arange(CHUNK, dtype=F32)
    mask = idx[:, None] >= idx[None, :]
    rel = jnp.where(mask, idx[:, None] - idx[None, :], 0.0)
    intra = (jnp.exp(rel[..., None] * log_gamma) * mask[..., None]).transpose(2, 0, 1)
    k_decay = jnp.repeat(jnp.exp((CHUNK - 1.0 - idx)[:, None] * log_gamma), C_HEAD_QK, axis=1)
    q_decay = jnp.repeat(jnp.exp((idx + 1.0)[:, None] * log_gamma), C_HEAD_QK, axis=1)
    chunk_decay = jnp.broadcast_to(jnp.repeat(jnp.exp(CHUNK * log_gamma), C_HEAD_QK)[:, None],
                                   (C_QK, LANES))
    return cos, sin, intra, k_decay, q_decay, chunk_decay


def _retention_body(qk_ref, v_ref, g_ref, cos_ref, sin_ref, intra_ref, kdec_ref, qdec_ref, cdec_ref,
                    small_ref, fb_ref, o_ref, cum_ref, cum_t_ref, state_ref, carry_ref):
    @pl.when(pl.program_id(0) == 0)
    def _():
        state_ref[...] = jnp.zeros_like(state_ref)
        carry_ref[...] = jnp.zeros_like(carry_ref)

    for b in range(qk_ref.shape[0]):
        _retention_chunk(qk_ref.at[b], v_ref.at[b], g_ref.at[b], cos_ref, sin_ref, intra_ref, kdec_ref, qdec_ref,
                         cdec_ref, small_ref.at[b], fb_ref, o_ref.at[b], cum_ref.at[b], cum_t_ref.at[b],
                         state_ref.at[b], carry_ref.at[b])


def _retention_chunk(qk_ref, v_ref, g_ref, cos_ref, sin_ref, intra_ref, kdec_ref, qdec_ref, cdec_ref,
                     small_ref, fb_ref, o_ref, cum_ref, cum_t_ref, state_ref, carry_ref):
    log_f = -_softplus(-(small_ref[...] + fb_ref[...]))
    cum = _mask_dot_f32(_lower_tri(CHUNK), log_f) + carry_ref[...]
    carry_ref[...] = cum[CHUNK - 1:CHUNK, :]
    cum_ref[...] = cum
    cum_t_ref[...] = cum.T

    lane = lax.broadcasted_iota(jnp.int32, (CHUNK, LANES), 1)
    first_half = (lane % C_HEAD_QK) < (C_HEAD_QK // 2)
    low_head = lane < C_HEAD_QK
    low_rows = lax.broadcasted_iota(jnp.int32, (LANES, LANES), 0) < C_HEAD_QK
    cos = cos_ref[...]
    sin = sin_ref[...]

    def rotary(x):
        partner = jnp.where(first_half, pltpu.roll(x, LANES - C_HEAD_QK // 2, 1), pltpu.roll(x, C_HEAD_QK // 2, 1))
        return x * cos + partner * sin

    for pair in range(C_HEADS // 2):
        pc = slice(pair * LANES, (pair + 1) * LANES)
        q2 = rotary(qk_ref[:, pc].astype(F32))
        k2 = rotary(qk_ref[:, C_QK + pair * LANES:C_QK + (pair + 1) * LANES].astype(F32)) * (C_HEAD_QK ** -0.5)
        qd2 = q2 * qdec_ref[:, pc]
        kd2_t = (k2 * kdec_ref[:, pc]).T
        prev = state_ref[pc, :]
        new_state = []
        for half in range(2):
            h = 2 * pair + half
            hc = slice(h * LANES, (h + 1) * LANES)
            sel = low_head if half == 0 else ~low_head
            scores = _dot_nt(jnp.where(sel, q2, 0.0), k2) * intra_ref[h]
            vh = v_ref[:, hc]
            y = _dot(scores, vh) + _dot(jnp.where(sel, qd2, 0.0), prev)
            mu = jnp.mean(y, axis=-1, keepdims=True)
            var = jnp.mean(jnp.square(y - mu), axis=-1, keepdims=True)
            y = (y - mu) * lax.rsqrt(var + EPS)
            o_ref[:, hc] = (_silu(g_ref[:, hc].astype(F32)) * y).astype(o_ref.dtype)
            new_state.append(_dot(kd2_t, vh))
        state_ref[pc, :] = prev * cdec_ref[pc, :] + jnp.where(low_rows, new_state[0], new_state[1])


def retention(p_cd, p_small, fb_row, tables, bsz, seq):
    m = bsz * seq
    cos, sin, intra, k_decay, q_decay, chunk_decay = tables
    const2 = lambda c: (0, 0)
    p3 = p_cd.reshape(bsz, seq, p_cd.shape[1])
    y, cum, cum_t = pl.pallas_call(
        _retention_body,
        out_shape=[jax.ShapeDtypeStruct((bsz, seq, WIDTH), BF16), jax.ShapeDtypeStruct((bsz, seq, LANES), F32),
                   jax.ShapeDtypeStruct((bsz, LANES, seq), F32)],
        grid=(seq // CHUNK,),
        in_specs=[
            pl.BlockSpec((bsz, CHUNK, 2 * C_QK), lambda c: (0, c, 0)),
            pl.BlockSpec((bsz, CHUNK, WIDTH), lambda c: (0, c, 1)),
            pl.BlockSpec((bsz, CHUNK, WIDTH), lambda c: (0, c, 2)),
            pl.BlockSpec((CHUNK, LANES), lambda c: (c, 0)),
            pl.BlockSpec((CHUNK, LANES), lambda c: (c, 0)),
            pl.BlockSpec((C_HEADS, CHUNK, CHUNK), lambda c: (0, 0, 0)),
            pl.BlockSpec((CHUNK, C_QK), const2),
            pl.BlockSpec((CHUNK, C_QK), const2),
            pl.BlockSpec((C_QK, LANES), const2),
            pl.BlockSpec((bsz, CHUNK, LANES), lambda c: (0, c, 0)),
            pl.BlockSpec((1, LANES), const2),
        ],
        out_specs=[pl.BlockSpec((bsz, CHUNK, WIDTH), lambda c: (0, c, 0)),
                   pl.BlockSpec((bsz, CHUNK, LANES), lambda c: (0, c, 0)),
                   pl.BlockSpec((bsz, LANES, CHUNK), lambda c: (0, 0, c))],
        scratch_shapes=[pltpu.VMEM((bsz, C_QK, LANES), F32), pltpu.VMEM((bsz, 1, LANES), F32)],
        compiler_params=_params("arbitrary"),
        name="retention",
    )(p3, p3, p3, cos, sin, intra, k_decay, q_decay, chunk_decay, p_small.reshape(bsz, seq, LANES), fb_row)
    return y.reshape(m, WIDTH), cum.reshape(m, LANES), cum_t


def _fox_body(q_ref, k_ref, v_ref, cum_ref, cum_t_ref, o_ref, *, blk, hps):
    hg = pl.program_id(1)
    qi = pl.program_id(2)
    lane = lax.broadcasted_iota(jnp.int32, (blk, LANES), 1)
    scale = LOG2E * D_HEAD_DIM ** -0.5
    cols = [slice(i * D_HEAD_DIM, (i + 1) * D_HEAD_DIM) for i in range(hps)]
    q = [q_ref[:, c].astype(BF16) for c in cols]
    cq = [LOG2E * jnp.sum(jnp.where(lane == SMALL_FL + hg * hps + i, cum_ref[...], 0.0), axis=-1, keepdims=True)
          for i in range(hps)]

    def scores(i, kj):
        ks = pl.multiple_of(kj * blk, blk)
        s = _dot_nt(q[i], k_ref[pl.ds(ks, blk), cols[i]]) * scale
        return s + cq[i] - LOG2E * cum_t_ref[pl.ds(SMALL_FL + hg * hps + i, 1), pl.ds(ks, blk)], ks

    def update(i, carry, s, ks):
        m_prev, l_prev, acc = carry
        m_new = jnp.maximum(m_prev, jnp.max(s, axis=-1, keepdims=True))
        alpha = jnp.exp2(m_prev - m_new)
        p = jnp.exp2(s - m_new)
        l_new = alpha * l_prev + jnp.sum(p, axis=-1, keepdims=True)
        acc = alpha * acc + _dot(p, v_ref[pl.ds(ks, blk), cols[i]])
        return m_new, l_new, acc

    def full_block(kj, carries):
        return tuple(update(i, carries[i], *scores(i, kj)) for i in range(hps))

    init = (jnp.full((blk, 1), -jnp.inf, F32), jnp.zeros((blk, 1), F32), jnp.zeros((blk, D_HEAD_DIM), F32))
    carries = lax.fori_loop(0, qi, full_block, (init,) * hps)
    causal = _lower_tri(blk)
    for i in range(hps):
        s, ks = scores(i, qi)
        _, l_fin, acc = update(i, carries[i], jnp.where(causal, s, -jnp.inf), ks)
        o_ref[:, cols[i]] = (acc / l_fin).astype(o_ref.dtype)


def fox(p_cd, cum, cum_t, bsz, seq, blk=512, hps=4):
    nq = seq // blk
    width = hps * D_HEAD_DIM
    col0 = (2 * C_QK + 2 * WIDTH) // width
    ng = D_HEADS // hps
    return pl.pallas_call(
        functools.partial(_fox_body, blk=blk, hps=hps),
        out_shape=jax.ShapeDtypeStruct((bsz * seq, WIDTH), BF16),
        grid=(bsz, ng, nq),
        in_specs=[
            pl.BlockSpec((blk, width), lambda b, g, i: (b * nq + i, col0 + g)),
            pl.BlockSpec((seq, width), lambda b, g, i: (b, col0 + ng + g)),
            pl.BlockSpec((seq, width), lambda b, g, i: (b, col0 + 2 * ng + g)),
            pl.BlockSpec((blk, LANES), lambda b, g, i: (b * nq + i, 0)),
            pl.BlockSpec((None, 8, seq), lambda b, g, i: (b, SMALL_FL // 8, 0)),
        ],
        out_specs=pl.BlockSpec((blk, width), lambda b, g, i: (b * nq + i, g)),
        compiler_params=_params("parallel", "parallel", "arbitrary"),
        name="fox",
    )(p_cd, p_cd, p_cd, cum, cum_t)


def _merge_body(ya_ref, yb_ref, yc_ref, yd_ref, ga_ref, gb_ref, gc_ref, gd_ref, wb_ref, o_ref):
    merged = ga_ref[...].astype(F32) * _dot(ya_ref[...], wb_ref[0])
    merged = merged + gb_ref[...].astype(F32) * _dot(yb_ref[...], wb_ref[1])
    merged = merged + gc_ref[...].astype(F32) * _dot(yc_ref[...], wb_ref[2])
    merged = merged + gd_ref[...].astype(F32) * _dot(yd_ref[...], wb_ref[3])
    o_ref[...] = merged.astype(o_ref.dtype)


def merge(ys, gates, w_branch, layer, tm=1024, tn=512):
    m = ys[0].shape[0]
    d = w_branch.shape[-1]
    nj = d // tn
    y_spec = pl.BlockSpec((tm, WIDTH), lambda i, j: (i, 0))
    gate_specs = [pl.BlockSpec((tm, tn), functools.partial(lambda i, j, n: (i, n * nj + j), n=n)) for n in range(4)]
    return pl.pallas_call(
        _merge_body,
        out_shape=jax.ShapeDtypeStruct((m, d), BF16),
        grid=(m // tm, nj),
        in_specs=[y_spec] * 4 + gate_specs + [pl.BlockSpec((None, 4, WIDTH, tn), lambda i, j: (layer, 0, 0, j))],
        out_specs=pl.BlockSpec((tm, tn), lambda i, j: (i, j)),
        compiler_params=_params("parallel", "arbitrary"),
        name="merge",
    )(*ys, gates, gates, gates, gates, w_branch)


def _out_proj_body(a_ref, x_ref, gn_ref, w_ref, xo_ref, ho_ref):
    x_new = x_ref[...] + _dot(a_ref[...], w_ref[...])
    xo_ref[...] = x_new
    ho_ref[...] = _rmsnorm_rows(x_new, gn_ref[...]).astype(ho_ref.dtype)


def out_proj(a, x, w, layer, g_next, tm=512):
    m, d = x.shape
    k = a.shape[1]
    row_spec = pl.BlockSpec((tm, d), lambda i: (i, 0))
    return pl.pallas_call(
        _out_proj_body,
        out_shape=[jax.ShapeDtypeStruct((m, d), F32), jax.ShapeDtypeStruct((m, d), BF16)],
        grid=(m // tm,),
        in_specs=[
            pl.BlockSpec((tm, k), lambda i: (i, 0)),
            row_spec,
            pl.BlockSpec((1, d), lambda i: (0, 0)),
            pl.BlockSpec((None, k, d), lambda i: (layer, 0, 0), pipeline_mode=pl.Buffered(1)),
        ],
        out_specs=[row_spec, row_spec],
        compiler_params=_params("parallel"),
        name="out_proj",
    )(a, x, g_next.reshape(1, d), w)


def _pad_lanes(v, offset):
    return jnp.zeros((1, LANES), F32).at[0, offset:offset + v.shape[0]].set(v.astype(F32))


def kernel(x, ffn1_norm, ffn1_w_in, ffn1_w_out, mix_norm, w_mix_in, sgu_norm, sgu_w, sgu_b, conv_w, conv_b,
           dt_bias, a_log, d_skip, ssm_norm, forget_bias, w_branch, w_mix_out, ffn2_norm, ffn2_w_in, ffn2_w_out,
           final_norm):
    bsz, seq, d = x.shape
    depth = ffn1_norm.shape[0]
    m = bsz * seq
    x = x.reshape(m, d)
    tables = _retention_tables(seq)
    w_t = jnp.swapaxes(w_mix_in, 1, 2)
    h = rmsnorm(x, ffn1_norm[0], BF16)
    for l in range(depth):
        x, h = ffn(h, x, ffn1_w_in, ffn1_w_out, l, mix_norm[l], emit_x=True, h_dtype=BF16)

        p_ab = proj(h, w_t, l, 0, N_AB, tn=768)
        p_small = proj_small(h, w_t, l)
        p_cd = proj(h, w_t, l, COL_C, N_CD, tn=1024)
        gates = proj(h, w_t, l, COL_G, 4 * d, act="sigmoid", tn=1024)

        b_full = jnp.repeat(sgu_b[l].T, LANES, axis=1)
        y_a = sgu(p_ab, sgu_norm[l], sgu_w, b_full, l)
        y_b = ssd(p_ab, p_small, conv_w, conv_b, _pad_lanes(dt_bias[l], SMALL_DT), _pad_lanes(a_log[l], SMALL_DT),
                  jnp.repeat(d_skip[l], B_HEAD_DIM).reshape(1, WIDTH), ssm_norm[l], l, bsz, seq)
        y_c, cum, cum_t = retention(p_cd, p_small, _pad_lanes(forget_bias[l], SMALL_FL), tables, bsz, seq)
        y_d = fox(p_cd, cum, cum_t, bsz, seq)

        merged = merge((y_a, y_b, y_c, y_d), gates, w_branch, l)
        x, h = out_proj(merged, x, w_mix_out, l, ffn2_norm[l])
        if l + 1 < depth:
            x, h = ffn(h, x, ffn2_w_in, ffn2_w_out, l, ffn1_norm[l + 1], emit_x=True, h_dtype=BF16)
        else:
            h = ffn(h, x, ffn2_w_in, ffn2_w_out, l, final_norm, emit_x=False, h_dtype=F32)
    return h.reshape(bsz, seq, d)
```

```python
import functools

import jax
import jax.numpy as jnp
from jax import lax
from jax.experimental import pallas as pl
from jax.experimental.pallas import tpu as pltpu

F32 = jnp.float32
BF16 = jnp.bfloat16

EPS = 1e-6
LOG2E = 1.4426950408889634
CHUNK = 128
LANES = 128
VMEM_LIMIT = 56 * 1024 * 1024

D_MODEL = 2048
D_FF = 5632
WIDTH = 1024
A_GROUPS = 8
B_HEADS = 16
B_HEAD_DIM = 64
B_STATE = 128
B_GROUPS = 2
B_CONV = 4
B_CONV_DIM = WIDTH + 2 * B_GROUPS * B_STATE
C_HEADS = 8
C_HEAD_QK = 64
C_QK = C_HEADS * C_HEAD_QK
D_HEADS = 8
D_HEAD_DIM = 128
ROPE_BASE = 10000.0

COL_DT = 2 * WIDTH + WIDTH + B_CONV_DIM
COL_C = COL_DT + B_HEADS
COL_FL = COL_C + 2 * C_QK + 2 * WIDTH + 3 * WIDTH
COL_G = COL_FL + D_HEADS
N_AB = COL_DT
N_CD = COL_FL - COL_C
SMALL_FL = 0
SMALL_DT = D_HEADS


def _params(*sem):
    return pltpu.CompilerParams(dimension_semantics=sem, vmem_limit_bytes=VMEM_LIMIT)


def _dot(a, b):
    return jnp.dot(a.astype(BF16), b.astype(BF16), preferred_element_type=F32)


def _dot_nt(a, b):
    return lax.dot_general(a.astype(BF16), b.astype(BF16), (((1,), (1,)), ((), ())),
                           preferred_element_type=F32)


def _split3(x):
    hi = x.astype(BF16)
    rest = x - hi.astype(F32)
    mid = rest.astype(BF16)
    lo = (rest - mid.astype(F32)).astype(BF16)
    return lo, mid, hi


def _dot_f32_by_mask(x, mask01):
    mask01 = mask01.astype(BF16)
    lo, mid, hi = (jnp.dot(t, mask01, preferred_element_type=F32) for t in _split3(x))
    return lo + mid + hi


def _mask_dot_f32(mask01, x):
    mask01 = mask01.astype(BF16)
    lo, mid, hi = (jnp.dot(mask01, t, preferred_element_type=F32) for t in _split3(x))
    return lo + mid + hi


def _sigmoid(x):
    return 0.5 * jnp.tanh(0.5 * x) + 0.5


def _silu(x):
    return x * _sigmoid(x)


def _softplus(x):
    return jnp.maximum(x, 0.0) + jnp.log1p(jnp.exp(-jnp.abs(x)))


def _lower_tri(n):
    return lax.broadcasted_iota(jnp.int32, (n, n), 1) <= lax.broadcasted_iota(jnp.int32, (n, n), 0)


def _rmsnorm_rows(x, g):
    return x * lax.rsqrt(jnp.mean(x * x, axis=-1, keepdims=True) + EPS) * g


def _rmsnorm_body(x_ref, g_ref, o_ref):
    o_ref[...] = _rmsnorm_rows(x_ref[...], g_ref[...]).astype(o_ref.dtype)


def rmsnorm(x, g, out_dtype, tm=512):
    m, d = x.shape
    return pl.pallas_call(
        _rmsnorm_body,
        out_shape=jax.ShapeDtypeStruct((m, d), out_dtype),
        grid=(m // tm,),
        in_specs=[pl.BlockSpec((tm, d), lambda i: (i, 0)), pl.BlockSpec((1, d), lambda i: (0, 0))],
        out_specs=pl.BlockSpec((tm, d), lambda i: (i, 0)),
        compiler_params=_params("parallel"),
        name="rmsnorm",
    )(x, g.reshape(1, d))


def _residual_rows_copy(x_hbm, acc_ref, sem, tm):
    rows = pl.ds(pl.multiple_of(pl.program_id(0) * tm, tm), tm)
    return pltpu.make_async_copy(x_hbm.at[rows, :], acc_ref, sem)


def _ffn_body(h_ref, x_hbm, gn_ref, wg_ref, wu_ref, wo_ref, *refs, nf, tm, emit_x):
    xo_ref, ho_ref, sem = refs if emit_x else (refs[1], refs[0], refs[2])
    f = pl.program_id(1)

    @pl.when(f == 0)
    def _():
        _residual_rows_copy(x_hbm, xo_ref, sem, tm).start()

    tf = wg_ref.shape[1]
    gu = _dot(h_ref[...], jnp.concatenate([wg_ref[...].astype(BF16), wu_ref[...].astype(BF16)], axis=1))
    act = 0.5 * _silu(gu[:, :tf]) * gu[:, tf:]

    @pl.when(f == 0)
    def _():
        _residual_rows_copy(x_hbm, xo_ref, sem, tm).wait()

    xo_ref[...] += _dot(act, wo_ref[...])

    @pl.when(f == nf - 1)
    def _():
        ho_ref[...] = _rmsnorm_rows(xo_ref[...], gn_ref[...]).astype(ho_ref.dtype)


def ffn(h, x, w_in, w_out, layer, g_next, *, emit_x, h_dtype, tm=1024, tf=512):
    m, d = x.shape
    nf = D_FF // tf
    row_spec = pl.BlockSpec((tm, d), lambda i, f: (i, 0), pipeline_mode=pl.Buffered(1))
    out_shape = [jax.ShapeDtypeStruct((m, d), h_dtype)]
    out_specs = [row_spec]
    scratch = [pltpu.VMEM((tm, d), F32), pltpu.SemaphoreType.DMA(())]
    if emit_x:
        out_shape.insert(0, jax.ShapeDtypeStruct((m, d), F32))
        out_specs.insert(0, row_spec)
        scratch = scratch[1:]
    res = pl.pallas_call(
        functools.partial(_ffn_body, nf=nf, tm=tm, emit_x=emit_x),
        out_shape=out_shape,
        grid=(m // tm, nf),
        in_specs=[
            row_spec,
            pl.BlockSpec(memory_space=pl.ANY),
            pl.BlockSpec((1, d), lambda i, f: (0, 0)),
            pl.BlockSpec((None, d, tf), lambda i, f: (layer, 0, f)),
            pl.BlockSpec((None, d, tf), lambda i, f: (layer, 0, nf + f)),
            pl.BlockSpec((None, tf, d), lambda i, f: (layer, f, 0)),
        ],
        out_specs=out_specs,
        scratch_shapes=scratch,
        compiler_params=_params("parallel", "arbitrary"),
        name="ffn",
    )(h, x, g_next.reshape(1, d), w_in, w_in, w_out)
    return res if emit_x else res[0]


def _proj_body(h_ref, w_ref, o_ref, *, act):
    acc = _dot_nt(h_ref[...], w_ref[0])
    if act == "sigmoid":
        acc = _sigmoid(acc)
    o_ref[...] = acc.astype(o_ref.dtype)


def proj(h, w_t, layer, first_row, n, *, act=None, tm=2048, tn=512):
    m, k = h.shape
    assert first_row % 8 == 0 and n % tn == 0
    w_spec = pl.BlockSpec((pl.Element(1), pl.Element(tn), pl.Element(k)),
                          lambda i, j: (layer, pl.multiple_of(first_row + j * tn, 8), 0))
    return pl.pallas_call(
        functools.partial(_proj_body, act=act),
        out_shape=jax.ShapeDtypeStruct((m, n), BF16),
        grid=(m // tm, n // tn),
        in_specs=[pl.BlockSpec((tm, k), lambda i, j: (i, 0)), w_spec],
        out_specs=pl.BlockSpec((tm, tn), lambda i, j: (i, j)),
        compiler_params=_params("parallel", "arbitrary"),
        name="proj",
    )(h, w_t)


def _proj_small_body(h_ref, wf_ref, wd_ref, o_ref):
    k = h_ref.shape[1]
    pad = jnp.zeros((LANES - D_HEADS - B_HEADS, k), F32)
    w = jnp.concatenate([wf_ref[...], wd_ref[...], pad], axis=0)
    o_ref[...] = _dot_nt(h_ref[...], w)


def proj_small(h, w_t, layer, tm=2048):
    m, k = h.shape
    return pl.pallas_call(
        _proj_small_body,
        out_shape=jax.ShapeDtypeStruct((m, LANES), F32),
        grid=(m // tm,),
        in_specs=[
            pl.BlockSpec((tm, k), lambda i: (i, 0)),
            pl.BlockSpec((None, D_HEADS, k), lambda i: (layer, COL_FL // D_HEADS, 0)),
            pl.BlockSpec((None, B_HEADS, k), lambda i: (layer, COL_DT // B_HEADS, 0)),
        ],
        out_specs=pl.BlockSpec((tm, LANES), lambda i: (i, 0)),
        compiler_params=_params("parallel"),
        name="proj_small",
    )(h, w_t, w_t)


def _sgu_body(p_ref, lng_ref, w_ref, b_ref, o_ref):
    causal = _lower_tri(CHUNK)
    for ci in range(p_ref.shape[0] // CHUNK):
        rows = slice(ci * CHUNK, (ci + 1) * CHUNK)
        p = jax.nn.gelu(p_ref[rows, :].astype(F32))
        u = p[:, :WIDTH]
        v = p[:, WIDTH:]
        mu = jnp.mean(v, axis=-1, keepdims=True)
        var = jnp.mean(jnp.square(v - mu), axis=-1, keepdims=True)
        vn = (v - mu) * lax.rsqrt(var + EPS) * lng_ref[...]
        for g in range(A_GROUPS):
            cols = slice(g * LANES, (g + 1) * LANES)
            w = jnp.where(causal, w_ref[g], 0.0)
            mixed = _dot(w, vn[:, cols]) + b_ref[:, cols]
            o_ref[rows, cols] = (u[:, cols] * mixed).astype(o_ref.dtype)


def sgu(p_ab, ln_g, w_s, b_full, layer, rows=4 * CHUNK):
    m = p_ab.shape[0]
    return pl.pallas_call(
        _sgu_body,
        out_shape=jax.ShapeDtypeStruct((m, WIDTH), BF16),
        grid=(m // rows,),
        in_specs=[
            pl.BlockSpec((rows, 2 * WIDTH), lambda c: (c, 0)),
            pl.BlockSpec((1, WIDTH), lambda c: (0, 0)),
            pl.BlockSpec((None, A_GROUPS, CHUNK, CHUNK), lambda c: (layer, 0, 0, 0)),
            pl.BlockSpec((CHUNK, WIDTH), lambda c: (0, 0)),
        ],
        out_specs=pl.BlockSpec((rows, WIDTH), lambda c: (c, 0)),
        compiler_params=_params("parallel"),
        name="sgu",
    )(p_ab, ln_g.reshape(1, WIDTH), w_s, b_full)


def _ssd_body(z_ref, xbc_ref, small_ref, cw_ref, cb_ref, dtb_ref, alog_ref, dsk_ref, ng_ref, o_ref,
              xprev_ref, state_ref, y_ref):
    @pl.when(pl.program_id(0) == 0)
    def _():
        xprev_ref[...] = jnp.zeros_like(xprev_ref)
        state_ref[...] = jnp.zeros_like(state_ref)

    for b in range(z_ref.shape[0]):
        _ssd_chunk(z_ref.at[b], xbc_ref.at[b], small_ref.at[b], cw_ref, cb_ref, dtb_ref, alog_ref, dsk_ref, ng_ref,
                   o_ref.at[b], xprev_ref.at[b], state_ref.at[b], y_ref.at[b])


def _ssd_chunk(z_ref, xbc_ref, small_ref, cw_ref, cb_ref, dtb_ref, alog_ref, dsk_ref, ng_ref, o_ref,
               xprev_ref, state_ref, y_ref):
    xcur = xbc_ref[...]
    xold = xprev_ref[8:8 + CHUNK, :].astype(BF16)
    out_row = lax.broadcasted_iota(jnp.int32, (CHUNK, CHUNK), 0)
    src_row = lax.broadcasted_iota(jnp.int32, (CHUNK, CHUNK), 1)
    conv = cb_ref[...] + cw_ref[B_CONV - 1:B_CONV, :] * xcur.astype(F32)
    for j in range(B_CONV - 1):
        back = B_CONV - 1 - j
        from_cur = (src_row == out_row - back).astype(BF16)
        from_old = (src_row == out_row - back + CHUNK).astype(BF16)
        tap = jnp.dot(from_cur, xcur, preferred_element_type=F32) + jnp.dot(from_old, xold, preferred_element_type=F32)
        conv = conv + cw_ref[j:j + 1, :] * tap
    xprev_ref[8:8 + CHUNK, :] = xcur.astype(F32)
    xbc = _silu(conv)
    xs = xbc[:, :WIDTH]
    bm = xbc[:, WIDTH:WIDTH + B_GROUPS * B_STATE]
    cm = xbc[:, WIDTH + B_GROUPS * B_STATE:]

    dt = _softplus(small_ref[...] + dtb_ref[...])
    a = -jnp.exp(alog_ref[...])
    causal = _lower_tri(CHUNK)
    acum = _mask_dot_f32(causal, dt * a)
    acum_t = acum.T
    expand = (lax.broadcasted_iota(jnp.int32, (LANES, WIDTH), 0) - SMALL_DT
              == lax.broadcasted_iota(jnp.int32, (LANES, WIDTH), 1) // B_HEAD_DIM)
    dt_e = _dot_f32_by_mask(dt, expand)
    acum_e = _dot_f32_by_mask(acum, expand)
    eacum_e = jnp.exp(acum_e)
    dend_e = jnp.exp(acum_e[CHUNK - 1:CHUNK, :] - acum_e)
    xdt = xs * dt_e
    xend = xdt * dend_e
    lane = lax.broadcasted_iota(jnp.int32, (CHUNK, LANES), 1)
    low_half = lane < B_HEAD_DIM
    hpg = B_HEADS // B_GROUPS
    gw = WIDTH // B_GROUPS

    for g in range(B_GROUPS):
        bg = bm[:, g * B_STATE:(g + 1) * B_STATE]
        cg = cm[:, g * B_STATE:(g + 1) * B_STATE]
        cb = _dot_nt(cg, bg)
        gcols = slice(g * gw, (g + 1) * gw)
        prev = state_ref[:, gcols]
        y_off = _dot(cg, prev) * eacum_e[:, gcols]
        for pair in range(hpg // 2):
            pcols = slice(g * gw + pair * LANES, g * gw + (pair + 1) * LANES)
            x_pair = xdt[:, pcols]
            y_pair = jnp.zeros((CHUNK, LANES), F32)
            for half in range(2):
                hl = SMALL_DT + g * hpg + 2 * pair + half
                seg = acum[:, hl:hl + 1] - acum_t[hl:hl + 1, :]
                decay = jnp.exp(jnp.where(causal, seg, -jnp.inf))
                x_half = jnp.where(low_half if half == 0 else ~low_half, x_pair, 0.0)
                y_pair = y_pair + _dot(cb * decay, x_half)
            y_pair = y_pair + y_off[:, pair * LANES:(pair + 1) * LANES] + xs[:, pcols] * dsk_ref[:, pcols]
            y_ref[:, pcols] = y_pair
        st = _dot(bg.T, xend[:, gcols])
        state_ref[:, gcols] = prev * eacum_e[CHUNK - 1:CHUNK, gcols] + st

    y = y_ref[...] * _silu(z_ref[...].astype(F32))
    for g in range(B_GROUPS):
        gcols = slice(g * gw, (g + 1) * gw)
        yg = y[:, gcols]
        yg = yg * lax.rsqrt(jnp.mean(yg * yg, axis=-1, keepdims=True) + EPS)
        o_ref[:, gcols] = (yg * ng_ref[:, gcols]).astype(o_ref.dtype)


def ssd(p_ab, p_small, conv_w, conv_b, dtb_row, alog_row, dskip_row, norm_g, layer, bsz, seq):
    m = bsz * seq
    p3 = p_ab.reshape(bsz, seq, p_ab.shape[1])
    const2 = lambda c: (0, 0)
    y = pl.pallas_call(
        _ssd_body,
        out_shape=jax.ShapeDtypeStruct((bsz, seq, WIDTH), BF16),
        grid=(seq // CHUNK,),
        in_specs=[
            pl.BlockSpec((bsz, CHUNK, WIDTH), lambda c: (0, c, 2)),
            pl.BlockSpec((bsz, CHUNK, B_CONV_DIM), lambda c: (0, c, 2)),
            pl.BlockSpec((bsz, CHUNK, LANES), lambda c: (0, c, 0)),
            pl.BlockSpec((None, B_CONV, B_CONV_DIM), lambda c: (layer, 0, 0)),
            pl.BlockSpec((None, 1, B_CONV_DIM), lambda c: (layer, 0, 0)),
            pl.BlockSpec((1, LANES), const2),
            pl.BlockSpec((1, LANES), const2),
            pl.BlockSpec((1, WIDTH), const2),
            pl.BlockSpec((1, WIDTH), const2),
        ],
        out_specs=pl.BlockSpec((bsz, CHUNK, WIDTH), lambda c: (0, c, 0)),
        scratch_shapes=[pltpu.VMEM((bsz, CHUNK + 8, B_CONV_DIM), F32), pltpu.VMEM((bsz, B_STATE, WIDTH), F32),
                        pltpu.VMEM((bsz, CHUNK, WIDTH), F32)],
        compiler_params=_params("arbitrary"),
        name="ssd",
    )(p3, p3, p_small.reshape(bsz, seq, LANES), conv_w, conv_b.reshape(conv_b.shape[0], 1, B_CONV_DIM), dtb_row,
      alog_row, dskip_row, norm_g.reshape(1, WIDTH))
    return y.reshape(m, WIDTH)


def _retention_tables(seq):
    half = C_HEAD_QK // 2
    inv_freq = 1.0 / (ROPE_BASE ** (jnp.arange(half, dtype=F32) / half))
    ang = jnp.arange(seq, dtype=F32)[:, None] * inv_freq[None, :]
    cos = jnp.tile(jnp.cos(ang), (1, 2 * LANES // C_HEAD_QK))
    sin = jnp.tile(jnp.concatenate([-jnp.sin(ang), jnp.sin(ang)], axis=1), (1, LANES // C_HEAD_QK))
    log_gamma = jnp.log(1.0 - 2.0 ** (-5.0 - jnp.arange(C_HEADS, dtype=F32)))
    idx = jnp.arange(CHUNK, dtype=F32)
    mask = idx[:, None] >= idx[None, :]
    rel = jnp.where(mask, idx[:, None] - idx[None, :], 0.0)
    intra = (jnp.exp(rel[..., None] * log_gamma) * mask[..., None]).transpose(2, 0, 1)
    k_decay = jnp.repeat(jnp.exp((CHUNK - 1.0 - idx)[:, None] * log_gamma), C_HEAD_QK, axis=1)
    q_decay = jnp.repeat(jnp.exp((idx + 1.0)[:, None] * log_gamma), C_HEAD_QK, axis=1)
    chunk_decay = jnp.broadcast_to(jnp.repeat(jnp.exp(CHUNK * log_gamma), C_HEAD_QK)[:, None],
                                   (C_QK, LANES))
    return cos, sin, intra, k_decay, q_decay, chunk_decay


def _retention_body(qk_ref, v_ref, g_ref, cos_ref, sin_ref, intra_ref, kdec_ref, qdec_ref, cdec_ref,
                    small_ref, fb_ref, o_ref, cum_ref, cum_t_ref, state_ref, carry_ref):
    @pl.when(pl.program_id(0) == 0)
    def _():
        state_ref[...] = jnp.zeros_like(state_ref)
        carry_ref[...] = jnp.zeros_like(carry_ref)

    for b in range(qk_ref.shape[0]):
        _retention_chunk(qk_ref.at[b], v_ref.at[b], g_ref.at[b], cos_ref, sin_ref, intra_ref, kdec_ref, qdec_ref,
                         cdec_ref, small_ref.at[b], fb_ref, o_ref.at[b], cum_ref.at[b], cum_t_ref.at[b],
                         state_ref.at[b], carry_ref.at[b])


def _retention_chunk(qk_ref, v_ref, g_ref, cos_ref, sin_ref, intra_ref, kdec_ref, qdec_ref, cdec_ref,
                     small_ref, fb_ref, o_ref, cum_ref, cum_t_ref, state_ref, carry_ref):
    log_f = -_softplus(-(small_ref[...] + fb_ref[...]))
    cum = _mask_dot_f32(_lower_tri(CHUNK), log_f) + carry_ref[...]
    carry_ref[...] = cum[CHUNK - 1:CHUNK, :]
    cum_ref[...] = cum
    cum_t_ref[...] = cum.T

    lane = lax.broadcasted_iota(jnp.int32, (CHUNK, LANES), 1)
    first_half = (lane % C_HEAD_QK) < (C_HEAD_QK // 2)
    low_head = lane < C_HEAD_QK
    low_rows = lax.broadcasted_iota(jnp.int32, (LANES, LANES), 0) < C_HEAD_QK
    cos = cos_ref[...]
    sin = sin_ref[...]

    def rotary(x):
        partner = jnp.where(first_half, pltpu.roll(x, LANES - C_HEAD_QK // 2, 1), pltpu.roll(x, C_HEAD_QK // 2, 1))
        return x * cos + partner * sin

    for pair in range(C_HEADS // 2):
        pc = slice(pair * LANES, (pair + 1) * LANES)
        q2 = rotary(qk_ref[:, pc].astype(F32))
        k2 = rotary(qk_ref[:, C_QK + pair * LANES:C_QK + (pair + 1) * LANES].astype(F32)) * (C_HEAD_QK ** -0.5)
        qd2 = q2 * qdec_ref[:, pc]
        kd2_t = (k2 * kdec_ref[:, pc]).T
        prev = state_ref[pc, :]
        new_state = []
        for half in range(2):
            h = 2 * pair + half
            hc = slice(h * LANES, (h + 1) * LANES)
            sel = low_head if half == 0 else ~low_head
            scores = _dot_nt(jnp.where(sel, q2, 0.0), k2) * intra_ref[h]
            vh = v_ref[:, hc]
            y = _dot(scores, vh) + _dot(jnp.where(sel, qd2, 0.0), prev)
            mu = jnp.mean(y, axis=-1, keepdims=True)
            var = jnp.mean(jnp.square(y - mu), axis=-1, keepdims=True)
            y = (y - mu) * lax.rsqrt(var + EPS)
            o_ref[:, hc] = (_silu(g_ref[:, hc].astype(F32)) * y).astype(o_ref.dtype)
            new_state.append(_dot(kd2_t, vh))
        state_ref[pc, :] = prev * cdec_ref[pc, :] + jnp.where(low_rows, new_state[0], new_state[1])


def retention(p_cd, p_small, fb_row, tables, bsz, seq):
    m = bsz * seq
    cos, sin, intra, k_decay, q_decay, chunk_decay = tables
    const2 = lambda c: (0, 0)
    p3 = p_cd.reshape(bsz, seq, p_cd.shape[1])
    y, cum, cum_t = pl.pallas_call(
        _retention_body,
        out_shape=[jax.ShapeDtypeStruct((bsz, seq, WIDTH), BF16), jax.ShapeDtypeStruct((bsz, seq, LANES), F32),
                   jax.ShapeDtypeStruct((bsz, LANES, seq), F32)],
        grid=(seq // CHUNK,),
        in_specs=[
            pl.BlockSpec((bsz, CHUNK, 2 * C_QK), lambda c: (0, c, 0)),
            pl.BlockSpec((bsz, CHUNK, WIDTH), lambda c: (0, c, 1)),
            pl.BlockSpec((bsz, CHUNK, WIDTH), lambda c: (0, c, 2)),
            pl.BlockSpec((CHUNK, LANES), lambda c: (c, 0)),
            pl.BlockSpec((CHUNK, LANES), lambda c: (c, 0)),
            pl.BlockSpec((C_HEADS, CHUNK, CHUNK), lambda c: (0, 0, 0)),
            pl.BlockSpec((CHUNK, C_QK), const2),
            pl.BlockSpec((CHUNK, C_QK), const2),
            pl.BlockSpec((C_QK, LANES), const2),
            pl.BlockSpec((bsz, CHUNK, LANES), lambda c: (0, c, 0)),
            pl.BlockSpec((1, LANES), const2),
        ],
        out_specs=[pl.BlockSpec((bsz, CHUNK, WIDTH), lambda c: (0, c, 0)),
                   pl.BlockSpec((bsz, CHUNK, LANES), lambda c: (0, c, 0)),
                   pl.BlockSpec((bsz, LANES, CHUNK), lambda c: (0, 0, c))],
        scratch_shapes=[pltpu.VMEM((bsz, C_QK, LANES), F32), pltpu.VMEM((bsz, 1, LANES), F32)],
        compiler_params=_params("arbitrary"),
        name="retention",
    )(p3, p3, p3, cos, sin, intra, k_decay, q_decay, chunk_decay, p_small.reshape(bsz, seq, LANES), fb_row)
    return y.reshape(m, WIDTH), cum.reshape(m, LANES), cum_t


def _fox_body(q_ref, k_ref, v_ref, cum_ref, cum_t_ref, o_ref, *, blk, hps):
    hg = pl.program_id(1)
    qi = pl.program_id(2)
    lane = lax.broadcasted_iota(jnp.int32, (blk, LANES), 1)
    scale = LOG2E * D_HEAD_DIM ** -0.5
    cols = [slice(i * D_HEAD_DIM, (i + 1) * D_HEAD_DIM) for i in range(hps)]
    q = [q_ref[:, c].astype(BF16) for c in cols]
    cq = [LOG2E * jnp.sum(jnp.where(lane == SMALL_FL + hg * hps + i, cum_ref[...], 0.0), axis=-1, keepdims=True)
          for i in range(hps)]

    def scores(i, kj):
        ks = pl.multiple_of(kj * blk, blk)
        s = _dot_nt(q[i], k_ref[pl.ds(ks, blk), cols[i]]) * scale
        return s + cq[i] - LOG2E * cum_t_ref[pl.ds(SMALL_FL + hg * hps + i, 1), pl.ds(ks, blk)], ks

    def update(i, carry, s, ks):
        m_prev, l_prev, acc = carry
        m_new = jnp.maximum(m_prev, jnp.max(s, axis=-1, keepdims=True))
        alpha = jnp.exp2(m_prev - m_new)
        p = jnp.exp2(s - m_new)
        l_new = alpha * l_prev + jnp.sum(p, axis=-1, keepdims=True)
        acc = alpha * acc + _dot(p, v_ref[pl.ds(ks, blk), cols[i]])
        return m_new, l_new, acc

    def full_block(kj, carries):
        return tuple(update(i, carries[i], *scores(i, kj)) for i in range(hps))

    init = (jnp.full((blk, 1), -jnp.inf, F32), jnp.zeros((blk, 1), F32), jnp.zeros((blk, D_HEAD_DIM), F32))
    carries = lax.fori_loop(0, qi, full_block, (init,) * hps)
    causal = _lower_tri(blk)
    for i in range(hps):
        s, ks = scores(i, qi)
        _, l_fin, acc = update(i, carries[i], jnp.where(causal, s, -jnp.inf), ks)
        o_ref[:, cols[i]] = (acc / l_fin).astype(o_ref.dtype)


def fox(p_cd, cum, cum_t, bsz, seq, blk=512, hps=4):
    nq = seq // blk
    width = hps * D_HEAD_DIM
    col0 = (2 * C_QK + 2 * WIDTH) // width
    ng = D_HEADS // hps
    return pl.pallas_call(
        functools.partial(_fox_body, blk=blk, hps=hps),
        out_shape=jax.ShapeDtypeStruct((bsz * seq, WIDTH), BF16),
        grid=(bsz, ng, nq),
        in_specs=[
            pl.BlockSpec((blk, width), lambda b, g, i: (b * nq + i, col0 + g)),
            pl.BlockSpec((seq, width), lambda b, g, i: (b, col0 + ng + g)),
            pl.BlockSpec((seq, width), lambda b, g, i: (b, col0 + 2 * ng + g)),
            pl.BlockSpec((blk, LANES), lambda b, g, i: (b * nq + i, 0)),
            pl.BlockSpec((None, 8, seq), lambda b, g, i: (b, SMALL_FL // 8, 0)),
        ],
        out_specs=pl.BlockSpec((blk, width), lambda b, g, i: (b * nq + i, g)),
        compiler_params=_params("parallel", "parallel", "arbitrary"),
        name="fox",
    )(p_cd, p_cd, p_cd, cum, cum_t)


def _merge_body(ya_ref, yb_ref, yc_ref, yd_ref, ga_ref, gb_ref, gc_ref, gd_ref, wb_ref, o_ref):
    merged = ga_ref[...].astype(F32) * _dot(ya_ref[...], wb_ref[0])
    merged = merged + gb_ref[...].astype(F32) * _dot(yb_ref[...], wb_ref[1])
    merged = merged + gc_ref[...].astype(F32) * _dot(yc_ref[...], wb_ref[2])
    merged = merged + gd_ref[...].astype(F32) * _dot(yd_ref[...], wb_ref[3])
    o_ref[...] = merged.astype(o_ref.dtype)


def merge(ys, gates, w_branch, layer, tm=1024, tn=512):
    m = ys[0].shape[0]
    d = w_branch.shape[-1]
    nj = d // tn
    y_spec = pl.BlockSpec((tm, WIDTH), lambda i, j: (i, 0))
    gate_specs = [pl.BlockSpec((tm, tn), functools.partial(lambda i, j, n: (i, n * nj + j), n=n)) for n in range(4)]
    return pl.pallas_call(
        _merge_body,
        out_shape=jax.ShapeDtypeStruct((m, d), BF16),
        grid=(m // tm, nj),
        in_specs=[y_spec] * 4 + gate_specs + [pl.BlockSpec((None, 4, WIDTH, tn), lambda i, j: (layer, 0, 0, j))],
        out_specs=pl.BlockSpec((tm, tn), lambda i, j: (i, j)),
        compiler_params=_params("parallel", "arbitrary"),
        name="merge",
    )(*ys, gates, gates, gates, gates, w_branch)


def _out_proj_body(a_ref, x_ref, gn_ref, w_ref, xo_ref, ho_ref):
    x_new = x_ref[...] + _dot(a_ref[...], w_ref[...])
    xo_ref[...] = x_new
    ho_ref[...] = _rmsnorm_rows(x_new, gn_ref[...]).astype(ho_ref.dtype)


def out_proj(a, x, w, layer, g_next, tm=512):
    m, d = x.shape
    k = a.shape[1]
    row_spec = pl.BlockSpec((tm, d), lambda i: (i, 0))
    return pl.pallas_call(
        _out_proj_body,
        out_shape=[jax.ShapeDtypeStruct((m, d), F32), jax.ShapeDtypeStruct((m, d), BF16)],
        grid=(m // tm,),
        in_specs=[
            pl.BlockSpec((tm, k), lambda i: (i, 0)),
            row_spec,
            pl.BlockSpec((1, d), lambda i: (0, 0)),
            pl.BlockSpec((None, k, d), lambda i: (layer, 0, 0), pipeline_mode=pl.Buffered(1)),
        ],
        out_specs=[row_spec, row_spec],
        compiler_params=_params("parallel"),
        name="out_proj",
    )(a, x, g_next.reshape(1, d), w)


def _pad_lanes(v, offset):
    return jnp.zeros((1, LANES), F32).at[0, offset:offset + v.shape[0]].set(v.astype(F32))


def kernel(x, ffn1_norm, ffn1_w_in, ffn1_w_out, mix_norm, w_mix_in, sgu_norm, sgu_w, sgu_b, conv_w, conv_b,
           dt_bias, a_log, d_skip, ssm_norm, forget_bias, w_branch, w_mix_out, ffn2_norm, ffn2_w_in, ffn2_w_out,
           final_norm):
    bsz, seq, d = x.shape
    depth = ffn1_norm.shape[0]
    m = bsz * seq
    x = x.reshape(m, d)
    tables = _retention_tables(seq)
    w_t = jnp.swapaxes(w_mix_in, 1, 2)
    h = rmsnorm(x, ffn1_norm[0], BF16)
    for l in range(depth):
        x, h = ffn(h, x, ffn1_w_in, ffn1_w_out, l, mix_norm[l], emit_x=True, h_dtype=BF16)

        p_ab = proj(h, w_t, l, 0, N_AB, tn=768)
        p_small = proj_small(h, w_t, l)
        p_cd = proj(h, w_t, l, COL_C, N_CD, tn=1024)
        gates = proj(h, w_t, l, COL_G, 4 * d, act="sigmoid", tn=1024)

        b_full = jnp.repeat(sgu_b[l].T, LANES, axis=1)
        y_a = sgu(p_ab, sgu_norm[l], sgu_w, b_full, l)
        y_b = ssd(p_ab, p_small, conv_w, conv_b, _pad_lanes(dt_bias[l], SMALL_DT), _pad_lanes(a_log[l], SMALL_DT),
                  jnp.repeat(d_skip[l], B_HEAD_DIM).reshape(1, WIDTH), ssm_norm[l], l, bsz, seq)
        y_c, cum, cum_t = retention(p_cd, p_small, _pad_lanes(forget_bias[l], SMALL_FL), tables, bsz, seq)
        y_d = fox(p_cd, cum, cum_t, bsz, seq)

        merged = merge((y_a, y_b, y_c, y_d), gates, w_branch, l)
        x, h = out_proj(merged, x, w_mix_out, l, ffn2_norm[l])
        if l + 1 < depth:
            x, h = ffn(h, x, ffn2_w_in, ffn2_w_out, l, ffn1_norm[l + 1], emit_x=True, h_dtype=BF16)
        else:
            h = ffn(h, x, ffn2_w_in, ffn2_w_out, l, final_norm, emit_x=False, h_dtype=F32)
    return h.reshape(bsz, seq, d)
```
